```python
import math
import jax
import jax.numpy as jnp
from jax import lax
import numpy as np

D_MODEL = 1024
BATCH = 2
SEQ = 16384
DEPTH = 4

N_A_LAYERS = DEPTH // 2
N_B_LAYERS = DEPTH - N_A_LAYERS
EPS = 1e-6

GDN_HEADS = 6
GDN_DK = 128
GDN_DV = 128
GDN_QK_W = GDN_HEADS * GDN_DK
GDN_V_W = GDN_HEADS * GDN_DV
CONV_K = 4
CHUNK = 64

SWA_HEADS = 12
SWA_KV_HEADS = 2
SWA_DH = 64
SWA_GROUP = SWA_HEADS // SWA_KV_HEADS
SWA_Q_W = SWA_HEADS * SWA_DH
KV_W = SWA_KV_HEADS * SWA_DH
WINDOW = 128
SWA_BLOCK = 128
ROPE_THETA = 500000.0
ROT_DIM = SWA_DH // 4

MEM_LEN = 256
MEM_HEADS = 4
MEM_DH = 64
MEM_W = MEM_HEADS * MEM_DH

D_MIX = GDN_V_W + MEM_W
GDN_IN = 2 * GDN_QK_W + 2 * GDN_V_W + 2 * GDN_HEADS + MEM_W
SWA_IN = SWA_Q_W + MEM_W
D_FF = -(-8 * D_MODEL // (3 * 256)) * 256

kernel_name = "yoco_gdn_swa_sink_memory_trunk"


def rms_norm(x, g):
    xf = x.astype(jnp.float32)
    y = xf * lax.rsqrt(jnp.mean(xf * xf, axis=-1, keepdims=True) + EPS)
    return (y * g.astype(jnp.float32)).astype(x.dtype)


def l2_normalize(x):
    xf = x.astype(jnp.float32)
    return xf * lax.rsqrt(jnp.sum(xf * xf, axis=-1, keepdims=True) + EPS)


def rope_tables(positions):
    inv = ROPE_THETA ** (-jnp.arange(0, ROT_DIM, 2, dtype=jnp.float32) / ROT_DIM)
    ang = positions.astype(jnp.float32)[..., None] * inv
    return jnp.cos(ang), jnp.sin(ang)


def apply_partial_rope(x, cos, sin):
    half = ROT_DIM // 2
    xf = x.astype(jnp.float32)
    x1, x2 = xf[..., :half], xf[..., half:ROT_DIM]
    c, s = cos[:, :, None, :], sin[:, :, None, :]
    out = jnp.concatenate([x1 * c - x2 * s, x2 * c + x1 * s, xf[..., ROT_DIM:]], axis=-1)
    return out.astype(x.dtype)


def causal_depthwise_conv(x, w):
    c = x.shape[-1]
    return lax.conv_general_dilated(
        x, w[:, None, :].astype(x.dtype), window_strides=(1,), padding=[(CONV_K - 1, 0)],
        dimension_numbers=("NWC", "WIO", "NWC"), feature_group_count=c)


def swiglu(h, w_gate_up, w_down):
    gu = h @ w_gate_up
    return (jax.nn.silu(gu[..., :D_FF]) * gu[..., D_FF:]) @ w_down


def gated_delta_rule_chunked(q, k, v, g, beta):
    b_sz, s_len, n_h, dk = q.shape
    dv = v.shape[-1]
    n_ch = s_len // CHUNK

    def chunks(t):
        t = t.reshape((b_sz, n_ch, CHUNK, n_h) + t.shape[3:])
        return jnp.moveaxis(t, 3, 1)

    q = chunks(q) * (dk ** -0.5)
    k = chunks(k)
    v = chunks(v)
    beta = chunks(beta)
    gc = jnp.cumsum(chunks(g), axis=-1)
    tril = jnp.tril(jnp.ones((CHUNK, CHUNK), dtype=bool))
    strict = jnp.tril(jnp.ones((CHUNK, CHUNK), dtype=bool), -1)
    decay = jnp.exp(jnp.where(tril, gc[..., :, None] - gc[..., None, :], -jnp.inf))
    kb = k * beta[..., None]
    lower = jnp.where(strict, jnp.einsum("bhncd,bhnkd->bhnck", kb, k) * decay, 0.0)
    rhs = jnp.concatenate([v * beta[..., None], kb * jnp.exp(gc)[..., None]], axis=-1)
    sol = lax.linalg.triangular_solve(lower, rhs, left_side=True, lower=True, unit_diagonal=True)
    u, w = sol[..., :dv], sol[..., dv:]
    intra = jnp.einsum("bhncd,bhnkd->bhnck", q, k) * decay
    q_g = q * jnp.exp(gc)[..., None]
    k_g = k * jnp.exp(gc[..., -1:] - gc)[..., None]
    g_last = jnp.exp(gc[..., -1])
    xs = tuple(jnp.moveaxis(t, 2, 0) for t in (u, w, intra, q_g, k_g, g_last))

    def step(state, inp):
        u_n, w_n, a_n, qg_n, kg_n, gl_n = inp
        v_new = u_n - jnp.einsum("bhck,bhkv->bhcv", w_n, state)
        o_n = jnp.einsum("bhck,bhkv->bhcv", qg_n, state) + jnp.einsum("bhcs,bhsv->bhcv", a_n, v_new)
        state = state * gl_n[..., None, None] + jnp.einsum("bhck,bhcv->bhkv", kg_n, v_new)
        return state, o_n

    s0 = jnp.zeros((b_sz, n_h, dk, dv), jnp.float32)
    _, o = lax.scan(step, s0, xs)
    return jnp.transpose(o, (1, 0, 3, 2, 4)).reshape(b_sz, s_len, n_h, dv)


def gated_deltanet_mixer(h, w_in, conv_w, a_log, dt_bias, norm_g):
    b_sz, s_len, _ = h.shape
    proj = h @ w_in
    o1 = 2 * GDN_QK_W + GDN_V_W
    qkv = jax.nn.silu(causal_depthwise_conv(proj[..., :o1], conv_w))
    z = proj[..., o1:o1 + GDN_V_W]
    o2 = o1 + GDN_V_W
    b_logit = proj[..., o2:o2 + GDN_HEADS].astype(jnp.float32)
    a_logit = proj[..., o2 + GDN_HEADS:o2 + 2 * GDN_HEADS].astype(jnp.float32)
    mem_q = proj[..., o2 + 2 * GDN_HEADS:]
    q = l2_normalize(qkv[..., :GDN_QK_W].reshape(b_sz, s_len, GDN_HEADS, GDN_DK))
    k = l2_normalize(qkv[..., GDN_QK_W:2 * GDN_QK_W].reshape(b_sz, s_len, GDN_HEADS, GDN_DK))
    v = qkv[..., 2 * GDN_QK_W:].reshape(b_sz, s_len, GDN_HEADS, GDN_DV).astype(jnp.float32)
    beta = jax.nn.sigmoid(b_logit)
    g = -jnp.exp(a_log.astype(jnp.float32)) * jax.nn.softplus(a_logit + dt_bias.astype(jnp.float32))
    o = gated_delta_rule_chunked(q, k, v, g, beta)
    o = rms_norm(o, norm_g) * jax.nn.silu(z.reshape(b_sz, s_len, GDN_HEADS, GDN_DV).astype(jnp.float32))
    return o.reshape(b_sz, s_len, GDN_V_W).astype(h.dtype), mem_q


def sliding_window_attention(q, k, v, sinks):
    b_sz, s_len = q.shape[:2]
    nb = s_len // SWA_BLOCK
    qb = q.reshape(b_sz, nb, SWA_BLOCK, SWA_KV_HEADS, SWA_GROUP, SWA_DH)
    kb = k.reshape(b_sz, nb, SWA_BLOCK, SWA_KV_HEADS, SWA_DH)
    vb = v.reshape(b_sz, nb, SWA_BLOCK, SWA_KV_HEADS, SWA_DH)
    pad = jnp.zeros_like(kb[:, :1])
    kw = jnp.concatenate([jnp.concatenate([pad, kb[:, :-1]], axis=1), kb], axis=2)
    vw = jnp.concatenate([jnp.concatenate([pad, vb[:, :-1]], axis=1), vb], axis=2)
    s = jnp.einsum("bnqhgd,bnkhd->bnhgqk", qb, kw).astype(jnp.float32) * (SWA_DH ** -0.5)
    qi = jnp.arange(SWA_BLOCK)[:, None] + SWA_BLOCK
    ki = jnp.arange(2 * SWA_BLOCK)[None, :]
    diff = qi - ki
    band = (diff >= 0) & (diff < WINDOW)
    has_prev = jnp.arange(nb) > 0
    mask = band[None] & (has_prev[:, None, None] | (ki >= SWA_BLOCK)[None])
    s = jnp.where(mask[None, :, None, None], s, -jnp.inf)
    sink = sinks.astype(jnp.float32).reshape(SWA_KV_HEADS, SWA_GROUP)[None, None, :, :, None, None]
    m = jnp.maximum(jnp.max(s, axis=-1, keepdims=True), sink)
    p = jnp.exp(s - m)
    p = (p / (jnp.sum(p, axis=-1, keepdims=True) + jnp.exp(sink - m))).astype(v.dtype)
    o = jnp.einsum("bnhgqk,bnkhd->bnqhgd", p, vw)
    return o.reshape(b_sz, s_len, SWA_Q_W)


def memory_attention(q, mem_k, mem_v):
    s = jnp.einsum("bshd,bmhd->bhsm", q, mem_k).astype(jnp.float32) * (MEM_DH ** -0.5)
    p = jax.nn.softmax(s, axis=-1).astype(q.dtype)
    o = jnp.einsum("bhsm,bmhd->bshd", p, mem_v)
    return o.reshape(q.shape[0], q.shape[1], MEM_W)


def setup_inputs(seed: int = 0) -> dict:
    key = jax.random.key(seed)
    ks = jax.random.split(key, 24)
    f32 = jnp.float32

    def dense(k, shape, fan_in):
        return jax.random.normal(k, shape, f32) * (fan_in ** -0.5)

    def gain(k, shape):
        return 1.0 + 0.02 * jax.random.normal(k, shape, f32)

    x = jax.random.normal(ks[0], (BATCH, SEQ, D_MODEL), f32)
    mem = jax.random.normal(ks[1], (BATCH, MEM_LEN, D_MODEL), f32)
    positions = (jnp.arange(SEQ, dtype=jnp.int32)[None, :]
                 + jax.random.randint(ks[2], (BATCH, 1), 0, 4096, dtype=jnp.int32))
    dt0 = jnp.exp(jax.random.uniform(ks[15], (N_A_LAYERS, GDN_HEADS), f32,
                                     math.log(1e-3), math.log(1e-1)))
    return {
        "x": x,
        "mem": mem,
        "positions": positions,
        "ln_mix": gain(ks[3], (DEPTH, D_MODEL)),
        "ln_ffn": gain(ks[4], (DEPTH, D_MODEL)),
        "ln_mem": gain(ks[5], (D_MODEL,)),
        "w_mem_kv": dense(ks[6], (DEPTH, D_MODEL, 2 * MEM_W), D_MODEL),
        "w_out": dense(ks[7], (DEPTH, D_MIX, D_MODEL), D_MIX),
        "w_gate_up": dense(ks[8], (DEPTH, D_MODEL, 2 * D_FF), D_MODEL),
        "w_down": dense(ks[9], (DEPTH, D_FF, D_MODEL), D_FF),
        "gdn_w_in": dense(ks[10], (N_A_LAYERS, D_MODEL, GDN_IN), D_MODEL),
        "gdn_conv": dense(ks[11], (N_A_LAYERS, CONV_K, 2 * GDN_QK_W + GDN_V_W), CONV_K),
        "gdn_A_log": jnp.log(jax.random.uniform(ks[12], (N_A_LAYERS, GDN_HEADS), f32, 1.0, 16.0)),
        "gdn_dt_bias": dt0 + jnp.log(-jnp.expm1(-dt0)),
        "gdn_norm": gain(ks[13], (N_A_LAYERS, GDN_DV)),
        "swa_w_q": dense(ks[14], (N_B_LAYERS, D_MODEL, SWA_IN), D_MODEL),
        "swa_sinks": 0.5 * jax.random.normal(ks[16], (N_B_LAYERS, SWA_HEADS), f32),
        "ln_kv": gain(ks[17], (D_MODEL,)),
        "w_kv": dense(ks[18], (D_MODEL, 2 * KV_W), D_MODEL),
        "ln_final": gain(ks[19], (D_MODEL,)),
    }


def reference(x, mem, positions, ln_mix, ln_ffn, ln_mem, w_mem_kv, w_out, w_gate_up, w_down,
              gdn_w_in, gdn_conv, gdn_A_log, gdn_dt_bias, gdn_norm,
              swa_w_q, swa_sinks, ln_kv, w_kv, ln_final):
    b_sz, s_len, _ = x.shape
    cos, sin = rope_tables(positions)
    mem_n = rms_norm(mem, ln_mem)
    shared_k = None
    shared_v = None
    for layer in range(DEPTH):
        h = rms_norm(x, ln_mix[layer])
        mkv = mem_n @ w_mem_kv[layer]
        mem_k = mkv[..., :MEM_W].reshape(b_sz, MEM_LEN, MEM_HEADS, MEM_DH)
        mem_v = mkv[..., MEM_W:].reshape(b_sz, MEM_LEN, MEM_HEADS, MEM_DH)
        if layer < N_A_LAYERS:
            a = layer
            mix_out, mem_q = gated_deltanet_mixer(h, gdn_w_in[a], gdn_conv[a], gdn_A_log[a],
                                                  gdn_dt_bias[a], gdn_norm[a])
        else:
            bl = layer - N_A_LAYERS
            proj = h @ swa_w_q[bl]
            q = apply_partial_rope(proj[..., :SWA_Q_W].reshape(b_sz, s_len, SWA_HEADS, SWA_DH), cos, sin)
            mix_out = sliding_window_attention(q, shared_k, shared_v, swa_sinks[bl])
            mem_q = proj[..., SWA_Q_W:]
        mem_o = memory_attention(mem_q.reshape(b_sz, s_len, MEM_HEADS, MEM_DH), mem_k, mem_v)
        x = x + jnp.concatenate([mix_out.astype(x.dtype), mem_o.astype(x.dtype)], axis=-1) @ w_out[layer]
        x = x + swiglu(rms_norm(x, ln_ffn[layer]), w_gate_up[layer], w_down[layer])
        if layer == N_A_LAYERS - 1:
            kv = rms_norm(x, ln_kv) @ w_kv
            shared_k = apply_partial_rope(kv[..., :KV_W].reshape(b_sz, s_len, SWA_KV_HEADS, SWA_DH), cos, sin)
            shared_v = kv[..., KV_W:].reshape(b_sz, s_len, SWA_KV_HEADS, SWA_DH)
    return rms_norm(x, ln_final)
```

```python
import functools

import jax
import jax.numpy as jnp
from jax import lax
from jax.experimental import pallas as pl
from jax.experimental.pallas import tpu as pltpu

F32 = jnp.float32
BF16 = jnp.bfloat16

EPS = 1e-6
LANES = 128
CHUNK = 64
CONV_K = 4
GDN_HEADS = 6
GDN_D = 128
SWA_HEADS = 12
SWA_KV_HEADS = 2
SWA_DH = 64
SWA_BLOCK = 128
ROT_DIM = 16
ROPE_THETA = 500000.0
MEM_HEADS = 4
MEM_DH = 64
VMEM_LIMIT = 56 * 1024 * 1024


def _cparams(sem):
    return pltpu.CompilerParams(dimension_semantics=sem, vmem_limit_bytes=VMEM_LIMIT)


def _rms(x, g):
    ms = jnp.mean(x * x, axis=-1, keepdims=True)
    return x * lax.rsqrt(ms + EPS) * g


def _silu(x):
    return x * (1.0 / (1.0 + jnp.exp(-x)))


def _rope(x, cosp, sa, sb):
    half = ROT_DIM // 2
    return x * cosp + pltpu.roll(x, LANES - half, axis=1) * sa + pltpu.roll(x, half, axis=1) * sb


def _norm_matmul_kernel(x_ref, g_ref, w_ref, o_ref):
    h = _rms(x_ref[...], g_ref[...]).astype(BF16)
    o_ref[...] = jnp.dot(h, w_ref[...], preferred_element_type=F32)


def norm_matmul(x, g, w, *, tm):
    t, d = x.shape
    n = w.shape[1]
    return pl.pallas_call(
        _norm_matmul_kernel,
        grid=(t // tm,),
        in_specs=[pl.BlockSpec((tm, d), lambda i: (i, 0)),
                  pl.BlockSpec((1, d), lambda i: (0, 0)),
                  pl.BlockSpec((d, n), lambda i: (0, 0))],
        out_specs=pl.BlockSpec((tm, n), lambda i: (i, 0)),
        out_shape=jax.ShapeDtypeStruct((t, n), F32),
        compiler_params=_cparams(("parallel",)),
        name="norm_matmul",
    )(x, g.reshape(1, d), w)


def _kv_proj_kernel(x_ref, g_ref, w_ref, cos_ref, sa_ref, sb_ref, o_ref):
    h = _rms(x_ref[...], g_ref[...]).astype(BF16)
    kv = jnp.dot(h, w_ref[...], preferred_element_type=F32)
    o_ref[:, :LANES] = _rope(kv[:, :LANES], cos_ref[...], sa_ref[...], sb_ref[...])
    o_ref[:, LANES:] = kv[:, LANES:]


def kv_proj(x, g, w, cosp, sa, sb, *, tm):
    t, d = x.shape
    n = w.shape[1]
    row = lambda i: (i, 0)
    const = lambda i: (0, 0)
    return pl.pallas_call(
        _kv_proj_kernel,
        grid=(t // tm,),
        in_specs=[pl.BlockSpec((tm, d), row), pl.BlockSpec((1, d), const), pl.BlockSpec((d, n), const),
                  pl.BlockSpec((tm, LANES), row), pl.BlockSpec((tm, LANES), row),
                  pl.BlockSpec((tm, LANES), row)],
        out_specs=pl.BlockSpec((tm, n), row),
        out_shape=jax.ShapeDtypeStruct((t, n), F32),
        compiler_params=_cparams(("parallel",)),
        name="kv_proj",
    )(x, g.reshape(1, d), w, cosp, sa, sb)


def _level_masks():
    ri = lax.broadcasted_iota(jnp.int32, (CHUNK, CHUNK), 0)
    ci = lax.broadcasted_iota(jnp.int32, (CHUNK, CHUNK), 1)
    masks = []
    for l in range(1, 7):
        same = (ri >> l) == (ci >> l)
        diff_half = (ri >> (l - 1)) != (ci >> (l - 1))
        masks.append(same & diff_half & (ri > ci))
    return ri, ci, masks


def _unit_lower_inverse(lmat, ri, ci, masks):
    eye = (ri == ci).astype(F32)
    p = eye - jnp.where(masks[0], lmat, 0.0)
    for l in range(1, 6):
        e = jnp.where(masks[l], lmat, 0.0)
        pe = jnp.dot(p, e, preferred_element_type=F32)
        p = p - jnp.dot(pe, p, preferred_element_type=F32)
    return p


def _gdn_kernel(qkv_ref, z_ref, ba_ref, cw_ref, alog_ref, dtb_ref, ng_ref, o_ref,
                state_ref, carry_ref, *, rows):
    t = pl.program_id(1)
    n_chunks = rows // CHUNK
    hw = GDN_HEADS * GDN_D

    @pl.when(t == 0)
    def _():
        state_ref[...] = jnp.zeros_like(state_ref)
        carry_ref[...] = jnp.zeros_like(carry_ref)

    ri, ci, masks = _level_masks()
    tril = ri >= ci
    strict = ri > ci

    ba = ba_ref[...]
    beta_all = 1.0 / (1.0 + jnp.exp(-ba))
    sp_in = ba + dtb_ref[...]
    softplus = jnp.maximum(sp_in, 0.0) + jnp.log(1.0 + jnp.exp(-jnp.abs(sp_in)))
    g_all = -jnp.exp(alog_ref[...]) * softplus
    rowi = lax.broadcasted_iota(jnp.int32, (rows, LANES), 0) & (CHUNK - 1)
    gc_all = g_all
    s = 1
    while s < CHUNK:
        gc_all = gc_all + jnp.where(rowi >= s, pltpu.roll(gc_all, s, axis=0), 0.0)
        s *= 2

    row8 = lax.broadcasted_iota(jnp.int32, (8, GDN_D), 0)

    def conv_silu(col):
        x = qkv_ref[:, col:col + GDN_D]
        c8 = carry_ref[:, col:col + GDN_D]
        acc = x * cw_ref[CONV_K - 1:CONV_K, col:col + GDN_D]
        for k in range(1, CONV_K):
            xr = pltpu.roll(x, k, axis=0)
            cr = pltpu.roll(c8, k, axis=0)
            head = jnp.where(row8 < k, cr, xr[:8])
            sh = jnp.concatenate([head, xr[8:]], axis=0)
            acc = acc + sh * cw_ref[CONV_K - 1 - k:CONV_K - k, col:col + GDN_D]
        return _silu(acc)

    for h in range(GDN_HEADS):
        q = conv_silu(h * GDN_D)
        k = conv_silu(hw + h * GDN_D)
        v = conv_silu(2 * hw + h * GDN_D)
        q = q * lax.rsqrt(jnp.sum(q * q, axis=-1, keepdims=True) + EPS) * (GDN_D ** -0.5)
        k = k * lax.rsqrt(jnp.sum(k * k, axis=-1, keepdims=True) + EPS)
        beta = beta_all[:, h:h + 1]
        gc = gc_all[:, GDN_HEADS + h:GDN_HEADS + h + 1]
        eg = jnp.exp(gc)
        kb = k * beta
        vb = v * beta
        kbg = kb * eg
        qg = q * eg
        state = state_ref[h]
        for c in range(n_chunks):
            sl = slice(c * CHUNK, (c + 1) * CHUNK)
            gc_c = gc[sl]
            gc_row = jnp.transpose(jnp.broadcast_to(gc_c, (CHUNK, CHUNK)))
            decay = jnp.exp(jnp.where(tril, gc_c - gc_row, -jnp.inf))
            k_c = k[sl]
            kt = k_c.astype(BF16)
            dn = (((1,), (1,)), ((), ()))
            kk = lax.dot_general(kb[sl].astype(BF16), kt, dn, preferred_element_type=F32)
            qk = lax.dot_general(q[sl].astype(BF16), kt, dn, preferred_element_type=F32)
            lmat = jnp.where(strict, kk * decay, 0.0)
            tinv = _unit_lower_inverse(lmat, ri, ci, masks)
            rhs = jnp.concatenate([vb[sl], kbg[sl]], axis=1).astype(BF16)
            sol = jnp.dot(tinv.astype(BF16), rhs, preferred_element_type=F32)
            u = sol[:, :GDN_D]
            w = sol[:, GDN_D:]
            intra = qk * decay
            g_last = gc_c[CHUNK - 1:CHUNK]
            kg = k_c * jnp.exp(g_last - gc_c)
            wq = jnp.concatenate([w, qg[sl]], axis=0).astype(BF16)
            ws_qs = jnp.dot(wq, state.astype(BF16), preferred_element_type=F32)
            v_new = u - ws_qs[:CHUNK]
            vnb = v_new.astype(BF16)
            o_c = ws_qs[CHUNK:] + jnp.dot(intra.astype(BF16), vnb, preferred_element_type=F32)
            state = state * jnp.exp(g_last) + jnp.dot(
                jnp.transpose(kg).astype(BF16), vnb, preferred_element_type=F32)
            zg = _silu(z_ref[sl, h * GDN_D:(h + 1) * GDN_D])
            o_ref[sl, h * GDN_D:(h + 1) * GDN_D] = _rms(o_c, ng_ref[...]) * zg
        state_ref[h] = state

    carry_ref[...] = qkv_ref[rows - 8:rows, :]


def gdn(proj, conv_w, a_log_p, dt_bias_p, norm_g, *, batch, seq, rows):
    t = batch * seq
    nt = seq // rows
    hw = GDN_HEADS * GDN_D
    kern = functools.partial(_gdn_kernel, rows=rows)
    const = lambda b, i: (0, 0)
    return pl.pallas_call(
        kern,
        grid=(batch, nt),
        in_specs=[pl.BlockSpec((rows, 3 * hw), lambda b, i: (b * nt + i, 0)),
                  pl.BlockSpec((rows, hw), lambda b, i: (b * nt + i, 3)),
                  pl.BlockSpec((rows, LANES), lambda b, i: (b * nt + i, (4 * hw + 256) // LANES)),
                  pl.BlockSpec((CONV_K, 3 * hw), const),
                  pl.BlockSpec((1, LANES), const),
                  pl.BlockSpec((1, LANES), const),
                  pl.BlockSpec((1, GDN_D), const)],
        out_specs=pl.BlockSpec((rows, hw), lambda b, i: (b * nt + i, 0)),
        out_shape=jax.ShapeDtypeStruct((t, hw), F32),
        scratch_shapes=[pltpu.VMEM((GDN_HEADS, GDN_D, GDN_D), F32),
                        pltpu.VMEM((8, 3 * hw), F32)],
        compiler_params=_cparams(("arbitrary", "arbitrary")),
        name="gdn",
    )(proj, proj, proj, conv_w, a_log_p, dt_bias_p, norm_g.reshape(1, GDN_D))


def _swa_kernel(sink_ref, q_ref, cos_ref, sa_ref, sb_ref, kvc_ref, kvp_ref, o_ref, *, qrows):
    t = pl.program_id(1)
    nsub = qrows // SWA_BLOCK
    lane = lax.broadcasted_iota(jnp.int32, (1, LANES), 1)
    lo = lane < SWA_DH
    qi = lax.broadcasted_iota(jnp.int32, (SWA_BLOCK, 2 * SWA_BLOCK), 0)
    ki = lax.broadcasted_iota(jnp.int32, (SWA_BLOCK, 2 * SWA_BLOCK), 1)
    band = (ki > qi) & (ki <= qi + SWA_BLOCK)
    dn = (((1,), (1,)), ((), ()))
    pairs_per_kv = SWA_HEADS // SWA_KV_HEADS // 2

    for j in range(nsub):
        if j == 0:
            kv_win = jnp.concatenate([kvp_ref[...], kvc_ref[:SWA_BLOCK, :]], axis=0)
            mask = band & (ki >= jnp.where(t > 0, 0, SWA_BLOCK))
        else:
            kv_win = kvc_ref[(j - 1) * SWA_BLOCK:(j + 1) * SWA_BLOCK, :]
            mask = band
        kblk = kv_win[:, :LANES]
        vblk = kv_win[:, LANES:]
        kroll = pltpu.roll(kblk, SWA_DH, axis=1)
        vroll = pltpu.roll(vblk, SWA_DH, axis=1)
        kexp = [(jnp.where(lo, kblk, 0.0).astype(BF16), jnp.where(lo, 0.0, kroll).astype(BF16)),
                (jnp.where(lo, kroll, 0.0).astype(BF16), jnp.where(lo, 0.0, kblk).astype(BF16))]
        vexp = [(jnp.where(lo, vblk, 0.0).astype(BF16), jnp.where(lo, 0.0, vroll).astype(BF16)),
                (jnp.where(lo, vroll, 0.0).astype(BF16), jnp.where(lo, 0.0, vblk).astype(BF16))]
        rs = slice(j * SWA_BLOCK, (j + 1) * SWA_BLOCK)
        cosp, sa, sb = cos_ref[rs, :], sa_ref[rs, :], sb_ref[rs, :]
        for p in range(SWA_HEADS // 2):
            kvh = p // pairs_per_kv
            qp = _rope(q_ref[rs, p * LANES:(p + 1) * LANES], cosp, sa, sb)
            qp = (qp * (SWA_DH ** -0.5)).astype(BF16)
            acc = None
            for half in range(2):
                sink = sink_ref[2 * p + half]
                sc = lax.dot_general(qp, kexp[kvh][half], dn, preferred_element_type=F32)
                sc = jnp.where(mask, sc, -jnp.inf)
                m = jnp.maximum(jnp.max(sc, axis=-1, keepdims=True), sink)
                e = jnp.exp(sc - m)
                den = jnp.sum(e, axis=-1, keepdims=True) + jnp.exp(sink - m)
                pr = (e / den).astype(BF16)
                pv = jnp.dot(pr, vexp[kvh][half], preferred_element_type=F32)
                acc = pv if acc is None else acc + pv
            o_ref[rs, p * LANES:(p + 1) * LANES] = acc


def swa(proj, kv, sinks, cosp, sa, sb, *, batch, seq, qrows):
    t = batch * seq
    nt = seq // qrows
    qw = SWA_HEADS * SWA_DH
    nsub = qrows // SWA_BLOCK
    nblk = seq // SWA_BLOCK
    row = lambda b, i: (b * nt + i, 0)
    kern = functools.partial(_swa_kernel, qrows=qrows)
    return pl.pallas_call(
        kern,
        grid=(batch, nt),
        in_specs=[pl.BlockSpec(memory_space=pltpu.SMEM),
                  pl.BlockSpec((qrows, qw), row),
                  pl.BlockSpec((qrows, LANES), row), pl.BlockSpec((qrows, LANES), row),
                  pl.BlockSpec((qrows, LANES), row),
                  pl.BlockSpec((qrows, 2 * LANES), row),
                  pl.BlockSpec((SWA_BLOCK, 2 * LANES),
                               lambda b, i: (b * nblk + jnp.maximum(i * nsub - 1, 0), 0))],
        out_specs=pl.BlockSpec((qrows, qw), row),
        out_shape=jax.ShapeDtypeStruct((t, qw), F32),
        compiler_params=_cparams(("parallel", "arbitrary")),
        name="swa",
    )(sinks, proj, cosp, sa, sb, kv, kv)


def _post_mix_kernel(x_ref, mix_ref, mq_ref, mkv_ref, w_ref, o_ref):
    lane = lax.broadcasted_iota(jnp.int32, (1, LANES), 1)
    lo = lane < MEM_DH
    mw = MEM_HEADS * MEM_DH
    dn = (((1,), (1,)), ((), ()))
    outs = [mix_ref[...].astype(BF16)]
    for p in range(MEM_HEADS // 2):
        qp = (mq_ref[:, p * LANES:(p + 1) * LANES] * (MEM_DH ** -0.5)).astype(BF16)
        kp = mkv_ref[:, p * LANES:(p + 1) * LANES]
        vp = mkv_ref[:, mw + p * LANES:mw + (p + 1) * LANES]
        acc = None
        for half in range(2):
            sel = lo if half == 0 else jnp.logical_not(lo)
            kh = jnp.where(sel, kp, 0.0).astype(BF16)
            vh = jnp.where(sel, vp, 0.0).astype(BF16)
            sc = lax.dot_general(qp, kh, dn, preferred_element_type=F32)
            m = jnp.max(sc, axis=-1, keepdims=True)
            e = jnp.exp(sc - m)
            pr = (e / jnp.sum(e, axis=-1, keepdims=True)).astype(BF16)
            pv = jnp.dot(pr, vh, preferred_element_type=F32)
            acc = pv if acc is None else acc + pv
        outs.append(acc.astype(BF16))
    cat = jnp.concatenate(outs, axis=1)
    o_ref[...] = x_ref[...] + jnp.dot(cat, w_ref[...], preferred_element_type=F32)


def post_mix(x, mix, proj, mq_block, mkv, w_out, *, batch, seq, tm):
    t, d = x.shape
    nt = seq // tm
    mixw = mix.shape[1]
    mw = MEM_HEADS * MEM_DH
    mlen = mkv.shape[0] // batch
    row = lambda b, i: (b * nt + i, 0)
    return pl.pallas_call(
        _post_mix_kernel,
        grid=(batch, nt),
        in_specs=[pl.BlockSpec((tm, d), row),
                  pl.BlockSpec((tm, mixw), row),
                  pl.BlockSpec((tm, mw), lambda b, i: (b * nt + i, mq_block)),
                  pl.BlockSpec((mlen, 2 * mw), lambda b, i: (b, 0)),
                  pl.BlockSpec((mixw + mw, d), lambda b, i: (0, 0))],
        out_specs=pl.BlockSpec((tm, d), row),
        out_shape=jax.ShapeDtypeStruct((t, d), F32),
        compiler_params=_cparams(("parallel", "parallel")),
        name="post_mix",
    )(x, mix, proj, mkv, w_out)


def _ffn_kernel(x_ref, g_ref, wg_ref, wu_ref, wd_ref, o_ref, acc_ref, *, ff_tile):
    x = x_ref[...]
    h = _rms(x, g_ref[...]).astype(BF16)
    d_ff = wg_ref.shape[1]
    acc_ref[...] = x
    for j in range(d_ff // ff_tile):
        cs = slice(j * ff_tile, (j + 1) * ff_tile)
        gate = jnp.dot(h, wg_ref[:, cs], preferred_element_type=F32)
        up = jnp.dot(h, wu_ref[:, cs], preferred_element_type=F32)
        act = (_silu(gate) * up).astype(BF16)
        acc_ref[...] += jnp.dot(act, wd_ref[cs, :], preferred_element_type=F32)
    o_ref[...] = acc_ref[...]


def ffn(x, g, wg, wu, wd, *, tm, ff_tile):
    t, d = x.shape
    d_ff = wg.shape[1]
    const = lambda i: (0, 0)
    single = pl.Buffered(1)
    kern = functools.partial(_ffn_kernel, ff_tile=ff_tile)
    return pl.pallas_call(
        kern,
        grid=(t // tm,),
        in_specs=[pl.BlockSpec((tm, d), lambda i: (i, 0)),
                  pl.BlockSpec((1, d), const),
                  pl.BlockSpec((d, d_ff), const, pipeline_mode=single),
                  pl.BlockSpec((d, d_ff), const, pipeline_mode=single),
                  pl.BlockSpec((d_ff, d), const, pipeline_mode=single)],
        out_specs=pl.BlockSpec((tm, d), lambda i: (i, 0)),
        out_shape=jax.ShapeDtypeStruct((t, d), F32),
        scratch_shapes=[pltpu.VMEM((tm, d), F32)],
        compiler_params=_cparams(("parallel",)),
        name="ffn",
    )(x, g.reshape(1, d), wg, wu, wd)


def _final_norm_kernel(x_ref, g_ref, o_ref):
    o_ref[...] = _rms(x_ref[...], g_ref[...])


def final_norm(x, g, *, tm):
    t, d = x.shape
    return pl.pallas_call(
        _final_norm_kernel,
        grid=(t // tm,),
        in_specs=[pl.BlockSpec((tm, d), lambda i: (i, 0)), pl.BlockSpec((1, d), lambda i: (0, 0))],
        out_specs=pl.BlockSpec((tm, d), lambda i: (i, 0)),
        out_shape=jax.ShapeDtypeStruct((t, d), F32),
        compiler_params=_cparams(("parallel",)),
        name="final_norm",
    )(x, g.reshape(1, d))


def _rope_lane_tables(positions):
    half = ROT_DIM // 2
    inv = ROPE_THETA ** (-jnp.arange(0, ROT_DIM, 2, dtype=F32) / ROT_DIM)
    ang = positions.astype(F32).reshape(-1, 1) * inv
    cos, sin = jnp.cos(ang), jnp.sin(ang)
    t = ang.shape[0]
    ones = jnp.ones((t, SWA_DH - ROT_DIM), F32)
    zeros = jnp.zeros((t, SWA_DH - ROT_DIM), F32)
    z8 = jnp.zeros((t, half), F32)
    cosp = jnp.concatenate([cos, cos, ones], axis=1)
    sa = jnp.concatenate([-sin, z8, zeros], axis=1)
    sb = jnp.concatenate([z8, sin, zeros], axis=1)
    tile = lambda a: jnp.concatenate([a, a], axis=1)
    return tile(cosp), tile(sa), tile(sb)


def _pad_lanes(v, offset):
    out = jnp.zeros((1, LANES), F32)
    return lax.dynamic_update_slice(out, v.reshape(1, -1).astype(F32), (0, offset))


def kernel(x, mem, positions, ln_mix, ln_ffn, ln_mem, w_mem_kv, w_out, w_gate_up, w_down,
           gdn_w_in, gdn_conv, gdn_A_log, gdn_dt_bias, gdn_norm,
           swa_w_q, swa_sinks, ln_kv, w_kv, ln_final):
    batch, seq, d = x.shape
    depth = ln_mix.shape[0]
    n_a = gdn_w_in.shape[0]
    t = batch * seq
    mlen = mem.shape[1]
    hw = GDN_HEADS * GDN_D
    mw = MEM_HEADS * MEM_DH
    d_ff = w_down.shape[1]

    cosp, sa, sb = _rope_lane_tables(positions)
    xs = x.reshape(t, d)
    mem2 = mem.reshape(batch * mlen, d)
    kv = None
    for layer in range(depth):
        mkv = norm_matmul(mem2, ln_mem, w_mem_kv[layer].astype(BF16), tm=batch * mlen)
        if layer < n_a:
            a = layer
            w_in = gdn_w_in[a]
            o2 = 4 * hw
            w_r = jnp.concatenate(
                [w_in[:, :o2], w_in[:, o2 + 2 * GDN_HEADS:], w_in[:, o2:o2 + 2 * GDN_HEADS],
                 jnp.zeros((d, LANES - 2 * GDN_HEADS), F32)], axis=1).astype(BF16)
            proj = norm_matmul(xs, ln_mix[layer], w_r, tm=512)
            mix = gdn(proj, gdn_conv[a], _pad_lanes(gdn_A_log[a], GDN_HEADS),
                      _pad_lanes(gdn_dt_bias[a], GDN_HEADS), gdn_norm[a],
                      batch=batch, seq=seq, rows=128)
            mq_block = o2 // mw
        else:
            bl = layer - n_a
            proj = norm_matmul(xs, ln_mix[layer], swa_w_q[bl].astype(BF16), tm=512)
            mix = swa(proj, kv, swa_sinks[bl], cosp, sa, sb, batch=batch, seq=seq, qrows=512)
            mq_block = (SWA_HEADS * SWA_DH) // mw
        xs = post_mix(xs, mix, proj, mq_block, mkv, w_out[layer].astype(BF16),
                      batch=batch, seq=seq, tm=512)
        wgu = w_gate_up[layer].astype(BF16)
        xs = ffn(xs, ln_ffn[layer], wgu[:, :d_ff], wgu[:, d_ff:], w_down[layer].astype(BF16),
                 tm=512, ff_tile=256)
        if layer == n_a - 1:
            kv = kv_proj(xs, ln_kv, w_kv.astype(BF16), cosp, sa, sb, tm=512)
    out = final_norm(xs, ln_final, tm=512)
    return out.reshape(batch, seq, d)
```

```python
import functools

import jax
import jax.numpy as jnp
from jax import lax
from jax.experimental import pallas as pl
from jax.experimental.pallas import tpu as pltpu

F32 = jnp.float32
BF16 = jnp.bfloat16

EPS = 1e-6
LANES = 128
CHUNK = 64
CONV_K = 4
GDN_HEADS = 6
GDN_D = 128
SWA_HEADS = 12
SWA_KV_HEADS = 2
SWA_DH = 64
SWA_BLOCK = 128
ROT_DIM = 16
ROPE_THETA = 500000.0
MEM_HEADS = 4
MEM_DH = 64
VMEM_LIMIT = 56 * 1024 * 1024


def _cparams(sem):
    return pltpu.CompilerParams(dimension_semantics=sem, vmem_limit_bytes=VMEM_LIMIT)


def _rms(x, g):
    ms = jnp.mean(x * x, axis=-1, keepdims=True)
    return x * lax.rsqrt(ms + EPS) * g


def _silu(x):
    return x * (1.0 / (1.0 + jnp.exp(-x)))


def _rope(x, cosp, sa, sb):
    half = ROT_DIM // 2
    return x * cosp + pltpu.roll(x, LANES - half, axis=1) * sa + pltpu.roll(x, half, axis=1) * sb


def _norm_matmul_kernel(x_ref, g_ref, w_ref, o_ref):
    h = _rms(x_ref[...], g_ref[...]).astype(BF16)
    o_ref[...] = jnp.dot(h, w_ref[...], preferred_element_type=F32)


def norm_matmul(x, g, w, *, tm):
    t, d = x.shape
    n = w.shape[1]
    return pl.pallas_call(
        _norm_matmul_kernel,
        grid=(t // tm,),
        in_specs=[pl.BlockSpec((tm, d), lambda i: (i, 0)),
                  pl.BlockSpec((1, d), lambda i: (0, 0)),
                  pl.BlockSpec((d, n), lambda i: (0, 0))],
        out_specs=pl.BlockSpec((tm, n), lambda i: (i, 0)),
        out_shape=jax.ShapeDtypeStruct((t, n), F32),
        compiler_params=_cparams(("parallel",)),
        name="norm_matmul",
    )(x, g.reshape(1, d), w)


def _kv_proj_kernel(x_ref, g_ref, w_ref, cos_ref, sa_ref, sb_ref, o_ref):
    h = _rms(x_ref[...], g_ref[...]).astype(BF16)
    kv = jnp.dot(h, w_ref[...], preferred_element_type=F32)
    o_ref[:, :LANES] = _rope(kv[:, :LANES], cos_ref[...], sa_ref[...], sb_ref[...])
    o_ref[:, LANES:] = kv[:, LANES:]


def kv_proj(x, g, w, cosp, sa, sb, *, tm):
    t, d = x.shape
    n = w.shape[1]
    row = lambda i: (i, 0)
    const = lambda i: (0, 0)
    return pl.pallas_call(
        _kv_proj_kernel,
        grid=(t // tm,),
        in_specs=[pl.BlockSpec((tm, d), row), pl.BlockSpec((1, d), const), pl.BlockSpec((d, n), const),
                  pl.BlockSpec((tm, LANES), row), pl.BlockSpec((tm, LANES), row),
                  pl.BlockSpec((tm, LANES), row)],
        out_specs=pl.BlockSpec((tm, n), row),
        out_shape=jax.ShapeDtypeStruct((t, n), F32),
        compiler_params=_cparams(("parallel",)),
        name="kv_proj",
    )(x, g.reshape(1, d), w, cosp, sa, sb)


def _level_masks():
    ri = lax.broadcasted_iota(jnp.int32, (CHUNK, CHUNK), 0)
    ci = lax.broadcasted_iota(jnp.int32, (CHUNK, CHUNK), 1)
    masks = []
    for l in range(1, 7):
        same = (ri >> l) == (ci >> l)
        diff_half = (ri >> (l - 1)) != (ci >> (l - 1))
        masks.append(same & diff_half & (ri > ci))
    return ri, ci, masks


def _bmm(a, b):
    return jnp.einsum("gmk,gkn->gmn", a, b, preferred_element_type=F32)


def _unit_lower_inverse(lmat, ri, ci, masks):
    eye = (ri == ci).astype(F32)
    p = eye - jnp.where(masks[0], lmat, 0.0)
    for l in range(1, 6):
        e = jnp.where(masks[l], lmat, 0.0).astype(BF16)
        pb = p.astype(BF16)
        pe = _bmm(pb, e)
        p = p - _bmm(pe.astype(BF16), pb)
    return p


def _gdn_kernel(qkv_ref, z_ref, ba_ref, cw_ref, alog_ref, dtb_ref, ng_ref, o_ref,
                state_ref, carry_ref, *, rows):
    t = pl.program_id(1)
    nc = rows // CHUNK
    nh = GDN_HEADS
    hw = nh * GDN_D

    @pl.when(t == 0)
    def _():
        state_ref[...] = jnp.zeros_like(state_ref)
        carry_ref[...] = jnp.zeros_like(carry_ref)

    ri, ci, masks = _level_masks()
    tril = ri >= ci
    strict = ri > ci

    ba = ba_ref[...]
    beta_all = 1.0 / (1.0 + jnp.exp(-ba))
    sp_in = ba + dtb_ref[...]
    softplus = jnp.maximum(sp_in, 0.0) + jnp.log(1.0 + jnp.exp(-jnp.abs(sp_in)))
    g_all = -jnp.exp(alog_ref[...]) * softplus
    rowi = lax.broadcasted_iota(jnp.int32, (rows, LANES), 0) & (CHUNK - 1)
    gc_all = g_all
    s = 1
    while s < CHUNK:
        gc_all = gc_all + jnp.where(rowi >= s, pltpu.roll(gc_all, s, axis=0), 0.0)
        s *= 2
    gc_t = jnp.transpose(gc_all)

    row8 = lax.broadcasted_iota(jnp.int32, (8, GDN_D), 0)

    def conv_silu(col):
        x = qkv_ref[:, col:col + GDN_D]
        c8 = carry_ref[:, col:col + GDN_D]
        acc = x * cw_ref[CONV_K - 1:CONV_K, col:col + GDN_D]
        for k in range(1, CONV_K):
            xr = pltpu.roll(x, k, axis=0)
            cr = pltpu.roll(c8, k, axis=0)
            head = jnp.where(row8 < k, cr, xr[:8])
            sh = jnp.concatenate([head, xr[8:]], axis=0)
            acc = acc + sh * cw_ref[CONV_K - 1 - k:CONV_K - k, col:col + GDN_D]
        return _silu(acc)

    parts = {name: [] for name in ("q", "k", "kb", "rhs", "qg", "kg", "gcb", "gcrow", "glast")}
    for h in range(nh):
        q = conv_silu(h * GDN_D)
        k = conv_silu(hw + h * GDN_D)
        v = conv_silu(2 * hw + h * GDN_D)
        q = q * lax.rsqrt(jnp.sum(q * q, axis=-1, keepdims=True) + EPS) * (GDN_D ** -0.5)
        k = k * lax.rsqrt(jnp.sum(k * k, axis=-1, keepdims=True) + EPS)
        beta = jnp.broadcast_to(beta_all[:, h:h + 1], (rows, GDN_D))
        gc = jnp.broadcast_to(gc_all[:, nh + h:nh + h + 1], (rows, GDN_D))
        gc3 = gc.reshape(nc, CHUNK, GDN_D)
        glast = jnp.broadcast_to(gc3[:, CHUNK - 1:CHUNK, :], (nc, CHUNK, GDN_D))
        eg = jnp.exp(gc)
        kb = k * beta
        three = lambda a: a.reshape(nc, CHUNK, a.shape[-1])
        parts["q"].append(three(q.astype(BF16)))
        parts["k"].append(three(k.astype(BF16)))
        parts["kb"].append(three(kb.astype(BF16)))
        parts["rhs"].append(three(jnp.concatenate([v * beta, kb * eg], axis=1).astype(BF16)))
        parts["qg"].append(three((q * eg).astype(BF16)))
        parts["kg"].append(three(k) * jnp.exp(glast - gc3))
        parts["gcb"].append(gc3[:, :, :CHUNK])
        parts["glast"].append(glast[:, :1, :])
        for c in range(nc):
            parts["gcrow"].append(
                gc_t[nh + h:nh + h + 1, c * CHUNK:(c + 1) * CHUNK].reshape(1, 1, CHUNK))
    st = {name: jnp.concatenate(v, axis=0) for name, v in parts.items()}

    decay = jnp.exp(jnp.where(tril, st["gcb"] - st["gcrow"], -jnp.inf))
    kq = jnp.einsum("gmd,gnd->gmn", jnp.concatenate([st["kb"], st["q"]], axis=1), st["k"],
                    preferred_element_type=F32)
    lmat = jnp.where(strict, kq[:, :CHUNK] * decay, 0.0)
    intra = (kq[:, CHUNK:] * decay).astype(BF16)
    tinv = _unit_lower_inverse(lmat, ri, ci, masks)
    sol = _bmm(tinv.astype(BF16), st["rhs"])

    four = lambda a: a.reshape((nh, nc) + a.shape[1:])
    u4 = four(sol[:, :, :GDN_D])
    w4 = four(sol[:, :, GDN_D:].astype(BF16))
    qg4, kg4, intra4 = four(st["qg"]), four(st["kg"]), four(intra)
    egl4 = four(jnp.exp(st["glast"]))
    zg = _silu(z_ref[...])
    state = state_ref[...]
    for c in range(nc):
        wq = jnp.concatenate([w4[:, c], qg4[:, c]], axis=1)
        ws_qs = _bmm(wq, state.astype(BF16))
        v_new = (u4[:, c] - ws_qs[:, :CHUNK]).astype(BF16)
        o_c = ws_qs[:, CHUNK:] + _bmm(intra4[:, c], v_new)
        kgt = jnp.stack([jnp.transpose(kg4[h, c]) for h in range(nh)], axis=0).astype(BF16)
        state = state * egl4[:, c] + _bmm(kgt, v_new)
        sl = slice(c * CHUNK, (c + 1) * CHUNK)
        for h in range(nh):
            cs = slice(h * GDN_D, (h + 1) * GDN_D)
            o_ref[sl, cs] = _rms(o_c[h], ng_ref[...]) * zg[sl, cs]
    state_ref[...] = state
    carry_ref[...] = qkv_ref[rows - 8:rows, :]


def gdn(proj, conv_w, a_log_p, dt_bias_p, norm_g, *, batch, seq, rows):
    t = batch * seq
    nt = seq // rows
    hw = GDN_HEADS * GDN_D
    kern = functools.partial(_gdn_kernel, rows=rows)
    const = lambda b, i: (0, 0)
    return pl.pallas_call(
        kern,
        grid=(batch, nt),
        in_specs=[pl.BlockSpec((rows, 3 * hw), lambda b, i: (b * nt + i, 0)),
                  pl.BlockSpec((rows, hw), lambda b, i: (b * nt + i, 3)),
                  pl.BlockSpec((rows, LANES), lambda b, i: (b * nt + i, (4 * hw + 256) // LANES)),
                  pl.BlockSpec((CONV_K, 3 * hw), const),
                  pl.BlockSpec((1, LANES), const),
                  pl.BlockSpec((1, LANES), const),
                  pl.BlockSpec((1, GDN_D), const)],
        out_specs=pl.BlockSpec((rows, hw), lambda b, i: (b * nt + i, 0)),
        out_shape=jax.ShapeDtypeStruct((t, hw), F32),
        scratch_shapes=[pltpu.VMEM((GDN_HEADS, GDN_D, GDN_D), F32),
                        pltpu.VMEM((8, 3 * hw), F32)],
        compiler_params=_cparams(("arbitrary", "arbitrary")),
        name="gdn",
    )(proj, proj, proj, conv_w, a_log_p, dt_bias_p, norm_g.reshape(1, GDN_D))


def _swa_kernel(sink_ref, q_ref, cos_ref, sa_ref, sb_ref, kvc_ref, kvp_ref, o_ref, *, qrows):
    t = pl.program_id(1)
    nsub = qrows // SWA_BLOCK
    lane = lax.broadcasted_iota(jnp.int32, (1, LANES), 1)
    lo = lane < SWA_DH
    qi = lax.broadcasted_iota(jnp.int32, (SWA_BLOCK, 2 * SWA_BLOCK), 0)
    ki = lax.broadcasted_iota(jnp.int32, (SWA_BLOCK, 2 * SWA_BLOCK), 1)
    band = (ki > qi) & (ki <= qi + SWA_BLOCK)
    dn = (((1,), (1,)), ((), ()))
    pairs_per_kv = SWA_HEADS // SWA_KV_HEADS // 2

    for j in range(nsub):
        if j == 0:
            kv_win = jnp.concatenate([kvp_ref[...], kvc_ref[:SWA_BLOCK, :]], axis=0)
            mask = band & (ki >= jnp.where(t > 0, 0, SWA_BLOCK))
        else:
            kv_win = kvc_ref[(j - 1) * SWA_BLOCK:(j + 1) * SWA_BLOCK, :]
            mask = band
        kblk = kv_win[:, :LANES]
        vblk = kv_win[:, LANES:]
        kroll = pltpu.roll(kblk, SWA_DH, axis=1)
        vroll = pltpu.roll(vblk, SWA_DH, axis=1)
        kexp = [(jnp.where(lo, kblk, 0.0).astype(BF16), jnp.where(lo, 0.0, kroll).astype(BF16)),
                (jnp.where(lo, kroll, 0.0).astype(BF16), jnp.where(lo, 0.0, kblk).astype(BF16))]
        vexp = [(jnp.where(lo, vblk, 0.0).astype(BF16), jnp.where(lo, 0.0, vroll).astype(BF16)),
                (jnp.where(lo, vroll, 0.0).astype(BF16), jnp.where(lo, 0.0, vblk).astype(BF16))]
        rs = slice(j * SWA_BLOCK, (j + 1) * SWA_BLOCK)
        cosp, sa, sb = cos_ref[rs, :], sa_ref[rs, :], sb_ref[rs, :]
        for p in range(SWA_HEADS // 2):
            kvh = p // pairs_per_kv
            qp = _rope(q_ref[rs, p * LANES:(p + 1) * LANES], cosp, sa, sb)
            qp = (qp * (SWA_DH ** -0.5)).astype(BF16)
            acc = None
            for half in range(2):
                sink = sink_ref[2 * p + half]
                sc = lax.dot_general(qp, kexp[kvh][half], dn, preferred_element_type=F32)
                sc = jnp.where(mask, sc, -jnp.inf)
                m = jnp.maximum(jnp.max(sc, axis=-1, keepdims=True), sink)
                e = jnp.exp(sc - m)
                den = jnp.sum(e, axis=-1, keepdims=True) + jnp.exp(sink - m)
                pr = (e / den).astype(BF16)
                pv = jnp.dot(pr, vexp[kvh][half], preferred_element_type=F32)
                acc = pv if acc is None else acc + pv
            o_ref[rs, p * LANES:(p + 1) * LANES] = acc


def swa(proj, kv, sinks, cosp, sa, sb, *, batch, seq, qrows):
    t = batch * seq
    nt = seq // qrows
    qw = SWA_HEADS * SWA_DH
    nsub = qrows // SWA_BLOCK
    nblk = seq // SWA_BLOCK
    row = lambda b, i: (b * nt + i, 0)
    kern = functools.partial(_swa_kernel, qrows=qrows)
    return pl.pallas_call(
        kern,
        grid=(batch, nt),
        in_specs=[pl.BlockSpec(memory_space=pltpu.SMEM),
                  pl.BlockSpec((qrows, qw), row),
                  pl.BlockSpec((qrows, LANES), row), pl.BlockSpec((qrows, LANES), row),
                  pl.BlockSpec((qrows, LANES), row),
                  pl.BlockSpec((qrows, 2 * LANES), row),
                  pl.BlockSpec((SWA_BLOCK, 2 * LANES),
                               lambda b, i: (b * nblk + jnp.maximum(i * nsub - 1, 0), 0))],
        out_specs=pl.BlockSpec((qrows, qw), row),
        out_shape=jax.ShapeDtypeStruct((t, qw), F32),
        compiler_params=_cparams(("parallel", "arbitrary")),
        name="swa",
    )(sinks, proj, cosp, sa, sb, kv, kv)


def _post_mix_kernel(x_ref, mix_ref, mq_ref, mkv_ref, w_ref, o_ref):
    lane = lax.broadcasted_iota(jnp.int32, (1, LANES), 1)
    lo = lane < MEM_DH
    mw = MEM_HEADS * MEM_DH
    dn = (((1,), (1,)), ((), ()))
    outs = [mix_ref[...].astype(BF16)]
    for p in range(MEM_HEADS // 2):
        qp = (mq_ref[:, p * LANES:(p + 1) * LANES] * (MEM_DH ** -0.5)).astype(BF16)
        kp = mkv_ref[:, p * LANES:(p + 1) * LANES]
        vp = mkv_ref[:, mw + p * LANES:mw + (p + 1) * LANES]
        acc = None
        for half in range(2):
            sel = lo if half == 0 else jnp.logical_not(lo)
            kh = jnp.where(sel, kp, 0.0).astype(BF16)
            vh = jnp.where(sel, vp, 0.0).astype(BF16)
            sc = lax.dot_general(qp, kh, dn, preferred_element_type=F32)
            m = jnp.max(sc, axis=-1, keepdims=True)
            e = jnp.exp(sc - m)
            pr = (e / jnp.sum(e, axis=-1, keepdims=True)).astype(BF16)
            pv = jnp.dot(pr, vh, preferred_element_type=F32)
            acc = pv if acc is None else acc + pv
        outs.append(acc.astype(BF16))
    cat = jnp.concatenate(outs, axis=1)
    o_ref[...] = x_ref[...] + jnp.dot(cat, w_ref[...], preferred_element_type=F32)


def post_mix(x, mix, proj, mq_block, mkv, w_out, *, batch, seq, tm):
    t, d = x.shape
    nt = seq // tm
    mixw = mix.shape[1]
    mw = MEM_HEADS * MEM_DH
    mlen = mkv.shape[0] // batch
    row = lambda b, i: (b * nt + i, 0)
    return pl.pallas_call(
        _post_mix_kernel,
        grid=(batch, nt),
        in_specs=[pl.BlockSpec((tm, d), row),
                  pl.BlockSpec((tm, mixw), row),
                  pl.BlockSpec((tm, mw), lambda b, i: (b * nt + i, mq_block)),
                  pl.BlockSpec((mlen, 2 * mw), lambda b, i: (b, 0)),
                  pl.BlockSpec((mixw + mw, d), lambda b, i: (0, 0))],
        out_specs=pl.BlockSpec((tm, d), row),
        out_shape=jax.ShapeDtypeStruct((t, d), F32),
        compiler_params=_cparams(("parallel", "parallel")),
        name="post_mix",
    )(x, mix, proj, mkv, w_out)


def _ffn_kernel(x_ref, g_ref, wg_ref, wu_ref, wd_ref, o_ref, acc_ref, *, ff_tile):
    x = x_ref[...]
    h = _rms(x, g_ref[...]).astype(BF16)
    d_ff = wg_ref.shape[1]
    acc_ref[...] = x
    for j in range(d_ff // ff_tile):
        cs = slice(j * ff_tile, (j + 1) * ff_tile)
        gate = jnp.dot(h, wg_ref[:, cs], preferred_element_type=F32)
        up = jnp.dot(h, wu_ref[:, cs], preferred_element_type=F32)
        act = (_silu(gate) * up).astype(BF16)
        acc_ref[...] += jnp.dot(act, wd_ref[cs, :], preferred_element_type=F32)
    o_ref[...] = acc_ref[...]


def ffn(x, g, wg, wu, wd, *, tm, ff_tile):
    t, d = x.shape
    d_ff = wg.shape[1]
    const = lambda i: (0, 0)
    single = pl.Buffered(1)
    kern = functools.partial(_ffn_kernel, ff_tile=ff_tile)
    return pl.pallas_call(
        kern,
        grid=(t // tm,),
        in_specs=[pl.BlockSpec((tm, d), lambda i: (i, 0)),
                  pl.BlockSpec((1, d), const),
                  pl.BlockSpec((d, d_ff), const, pipeline_mode=single),
                  pl.BlockSpec((d, d_ff), const, pipeline_mode=single),
                  pl.BlockSpec((d_ff, d), const, pipeline_mode=single)],
        out_specs=pl.BlockSpec((tm, d), lambda i: (i, 0)),
        out_shape=jax.ShapeDtypeStruct((t, d), F32),
        scratch_shapes=[pltpu.VMEM((tm, d), F32)],
        compiler_params=_cparams(("parallel",)),
        name="ffn",
    )(x, g.reshape(1, d), wg, wu, wd)


def _final_norm_kernel(x_ref, g_ref, o_ref):
    o_ref[...] = _rms(x_ref[...], g_ref[...])


def final_norm(x, g, *, tm):
    t, d = x.shape
    return pl.pallas_call(
        _final_norm_kernel,
        grid=(t // tm,),
        in_specs=[pl.BlockSpec((tm, d), lambda i: (i, 0)), pl.BlockSpec((1, d), lambda i: (0, 0))],
        out_specs=pl.BlockSpec((tm, d), lambda i: (i, 0)),
        out_shape=jax.ShapeDtypeStruct((t, d), F32),
        compiler_params=_cparams(("parallel",)),
        name="final_norm",
    )(x, g.reshape(1, d))


def _rope_lane_tables(positions):
    half = ROT_DIM // 2
    inv = ROPE_THETA ** (-jnp.arange(0, ROT_DIM, 2, dtype=F32) / ROT_DIM)
    ang = positions.astype(F32).reshape(-1, 1) * inv
    cos, sin = jnp.cos(ang), jnp.sin(ang)
    t = ang.shape[0]
    ones = jnp.ones((t, SWA_DH - ROT_DIM), F32)
    zeros = jnp.zeros((t, SWA_DH - ROT_DIM), F32)
    z8 = jnp.zeros((t, half), F32)
    cosp = jnp.concatenate([cos, cos, ones], axis=1)
    sa = jnp.concatenate([-sin, z8, zeros], axis=1)
    sb = jnp.concatenate([z8, sin, zeros], axis=1)
    tile = lambda a: jnp.concatenate([a, a], axis=1)
    return tile(cosp), tile(sa), tile(sb)


def _pad_lanes(v, offset):
    out = jnp.zeros((1, LANES), F32)
    return lax.dynamic_update_slice(out, v.reshape(1, -1).astype(F32), (0, offset))


def kernel(x, mem, positions, ln_mix, ln_ffn, ln_mem, w_mem_kv, w_out, w_gate_up, w_down,
           gdn_w_in, gdn_conv, gdn_A_log, gdn_dt_bias, gdn_norm,
           swa_w_q, swa_sinks, ln_kv, w_kv, ln_final):
    batch, seq, d = x.shape
    depth = ln_mix.shape[0]
    n_a = gdn_w_in.shape[0]
    t = batch * seq
    mlen = mem.shape[1]
    hw = GDN_HEADS * GDN_D
    mw = MEM_HEADS * MEM_DH
    d_ff = w_down.shape[1]

    cosp, sa, sb = _rope_lane_tables(positions)
    xs = x.reshape(t, d)
    mem2 = mem.reshape(batch * mlen, d)
    kv = None
    for layer in range(depth):
        mkv = norm_matmul(mem2, ln_mem, w_mem_kv[layer].astype(BF16), tm=batch * mlen)
        if layer < n_a:
            a = layer
            w_in = gdn_w_in[a]
            o2 = 4 * hw
            w_r = jnp.concatenate(
                [w_in[:, :o2], w_in[:, o2 + 2 * GDN_HEADS:], w_in[:, o2:o2 + 2 * GDN_HEADS],
                 jnp.zeros((d, LANES - 2 * GDN_HEADS), F32)], axis=1).astype(BF16)
            proj = norm_matmul(xs, ln_mix[layer], w_r, tm=512)
            mix = gdn(proj, gdn_conv[a], _pad_lanes(gdn_A_log[a], GDN_HEADS),
                      _pad_lanes(gdn_dt_bias[a], GDN_HEADS), gdn_norm[a],
                      batch=batch, seq=seq, rows=256)
            mq_block = o2 // mw
        else:
            bl = layer - n_a
            proj = norm_matmul(xs, ln_mix[layer], swa_w_q[bl].astype(BF16), tm=512)
            mix = swa(proj, kv, swa_sinks[bl], cosp, sa, sb, batch=batch, seq=seq, qrows=512)
            mq_block = (SWA_HEADS * SWA_DH) // mw
        xs = post_mix(xs, mix, proj, mq_block, mkv, w_out[layer].astype(BF16),
                      batch=batch, seq=seq, tm=512)
        wgu = w_gate_up[layer].astype(BF16)
        xs = ffn(xs, ln_ffn[layer], wgu[:, :d_ff], wgu[:, d_ff:], w_down[layer].astype(BF16),
                 tm=512, ff_tile=256)
        if layer == n_a - 1:
            kv = kv_proj(xs, ln_kv, w_kv.astype(BF16), cosp, sa, sb, tm=512)
    out = final_norm(xs, ln_final, tm=512)
    return out.reshape(batch, seq, d)
```

```python
import functools

import jax
import jax.numpy as jnp
from jax import lax
from jax.experimental import pallas as pl
from jax.experimental.pallas import tpu as pltpu

F32 = jnp.float32
BF16 = jnp.bfloat16

EPS = 1e-6
LANES = 128
CHUNK = 64
CONV_K = 4
GDN_HEADS = 6
GDN_D = 128
SWA_HEADS = 12
SWA_KV_HEADS = 2
SWA_DH = 64
SWA_BLOCK = 128
ROT_DIM = 16
ROPE_THETA = 500000.0
MEM_HEADS = 4
MEM_DH = 64
VMEM_LIMIT = 56 * 1024 * 1024


def _cparams(sem):
    return pltpu.CompilerParams(dimension_semantics=sem, vmem_limit_bytes=VMEM_LIMIT)


def _rms(x, g):
    ms = jnp.mean(x * x, axis=-1, keepdims=True)
    return x * lax.rsqrt(ms + EPS) * g


def _sigmoid(x):
    return 0.5 + 0.5 * jnp.tanh(0.5 * x)


def _silu(x):
    h = 0.5 * x
    return h + h * jnp.tanh(h)


def _rope(x, cosp, sinp):
    half = ROT_DIM // 2
    lane = lax.broadcasted_iota(jnp.int32, (1, LANES), 1) & (SWA_DH - 1)
    partner = jnp.where(lane < half, pltpu.roll(x, LANES - half, axis=1), pltpu.roll(x, half, axis=1))
    return x * cosp + partner * sinp


def _norm_matmul_kernel(x_ref, g_ref, w_ref, o_ref):
    h = _rms(x_ref[...], g_ref[...]).astype(BF16)
    o_ref[...] = jnp.dot(h, w_ref[...], preferred_element_type=F32)


def norm_matmul(x, g, w, *, tm):
    t, d = x.shape
    n = w.shape[1]
    return pl.pallas_call(
        _norm_matmul_kernel,
        grid=(t // tm,),
        in_specs=[pl.BlockSpec((tm, d), lambda i: (i, 0)),
                  pl.BlockSpec((1, d), lambda i: (0, 0)),
                  pl.BlockSpec((d, n), lambda i: (0, 0))],
        out_specs=pl.BlockSpec((tm, n), lambda i: (i, 0)),
        out_shape=jax.ShapeDtypeStruct((t, n), F32),
        compiler_params=_cparams(("parallel",)),
        name="norm_matmul",
    )(x, g.reshape(1, d), w)


def _kv_proj_kernel(x_ref, g_ref, w_ref, cos_ref, sin_ref, o_ref):
    h = _rms(x_ref[...], g_ref[...]).astype(BF16)
    kv = jnp.dot(h, w_ref[...], preferred_element_type=F32)
    o_ref[:, :LANES] = _rope(kv[:, :LANES], cos_ref[...], sin_ref[...])
    o_ref[:, LANES:] = kv[:, LANES:]


def kv_proj(x, g, w, cosp, sinp, *, tm):
    t, d = x.shape
    n = w.shape[1]
    row = lambda i: (i, 0)
    const = lambda i: (0, 0)
    return pl.pallas_call(
        _kv_proj_kernel,
        grid=(t // tm,),
        in_specs=[pl.BlockSpec((tm, d), row), pl.BlockSpec((1, d), const), pl.BlockSpec((d, n), const),
                  pl.BlockSpec((tm, LANES), row), pl.BlockSpec((tm, LANES), row)],
        out_specs=pl.BlockSpec((tm, n), row),
        out_shape=jax.ShapeDtypeStruct((t, n), F32),
        compiler_params=_cparams(("parallel",)),
        name="kv_proj",
    )(x, g.reshape(1, d), w, cosp, sinp)


def _level_masks():
    ri = lax.broadcasted_iota(jnp.int32, (CHUNK, CHUNK), 0)
    ci = lax.broadcasted_iota(jnp.int32, (CHUNK, CHUNK), 1)
    masks = []
    for l in range(1, 7):
        same = (ri >> l) == (ci >> l)
        diff_half = (ri >> (l - 1)) != (ci >> (l - 1))
        masks.append(same & diff_half & (ri > ci))
    return ri, ci, masks


def _bmm(a, b):
    return jnp.einsum("gmk,gkn->gmn", a, b, preferred_element_type=F32)


def _unit_lower_inverse(lmat, ri, ci, masks):
    eye = (ri == ci).astype(F32)
    p = eye - jnp.where(masks[0], lmat, 0.0)
    for l in range(1, 6):
        e = jnp.where(masks[l], lmat, 0.0).astype(BF16)
        pb = p.astype(BF16)
        pe = _bmm(pb, e)
        p = p - _bmm(pe.astype(BF16), pb)
    return p


def _gdn_kernel(qkv_ref, z_ref, ba_ref, cw_ref, alog_ref, dtb_ref, ng_ref, o_ref,
                state_ref, carry_ref, *, rows):
    t = pl.program_id(1)
    nc = rows // CHUNK
    nh = GDN_HEADS
    hw = nh * GDN_D

    @pl.when(t == 0)
    def _():
        state_ref[...] = jnp.zeros_like(state_ref)
        carry_ref[...] = jnp.zeros_like(carry_ref)

    ri, ci, masks = _level_masks()
    tril = ri >= ci
    strict = ri > ci

    ba = ba_ref[...]
    beta_all = _sigmoid(ba)
    sp_in = ba + dtb_ref[...]
    softplus = jnp.maximum(sp_in, 0.0) + jnp.log(1.0 + jnp.exp(-jnp.abs(sp_in)))
    g_all = -jnp.exp(alog_ref[...]) * softplus
    ti = lax.broadcasted_iota(jnp.int32, (rows, rows), 0)
    tj = lax.broadcasted_iota(jnp.int32, (rows, rows), 1)
    tri = (((ti // CHUNK) == (tj // CHUNK)) & (tj <= ti)).astype(BF16)
    g_hi = g_all.astype(BF16)
    rem = g_all - g_hi.astype(F32)
    g_mid = rem.astype(BF16)
    g_lo = (rem - g_mid.astype(F32)).astype(BF16)
    gsum = jnp.dot(tri, jnp.concatenate([g_hi, g_mid, g_lo], axis=1), preferred_element_type=F32)
    gc_all = gsum[:, :LANES] + gsum[:, LANES:2 * LANES] + gsum[:, 2 * LANES:]
    gc_t = jnp.transpose(gc_all)

    row8 = lax.broadcasted_iota(jnp.int32, (8, GDN_D), 0)

    def conv_silu(col):
        x = qkv_ref[:, col:col + GDN_D]
        c8 = carry_ref[:, col:col + GDN_D]
        acc = x * cw_ref[CONV_K - 1:CONV_K, col:col + GDN_D]
        for k in range(1, CONV_K):
            xr = pltpu.roll(x, k, axis=0)
            cr = pltpu.roll(c8, k, axis=0)
            head = jnp.where(row8 < k, cr, xr[:8])
            sh = jnp.concatenate([head, xr[8:]], axis=0)
            acc = acc + sh * cw_ref[CONV_K - 1 - k:CONV_K - k, col:col + GDN_D]
        return _silu(acc)

    parts = {name: [] for name in ("q", "k", "kb", "rhs", "qg", "kg", "gcb", "gcrow", "glast")}
    for h in range(nh):
        q = conv_silu(h * GDN_D)
        k = conv_silu(hw + h * GDN_D)
        v = conv_silu(2 * hw + h * GDN_D)
        q = q * lax.rsqrt(jnp.sum(q * q, axis=-1, keepdims=True) + EPS) * (GDN_D ** -0.5)
        k = k * lax.rsqrt(jnp.sum(k * k, axis=-1, keepdims=True) + EPS)
        beta = jnp.broadcast_to(beta_all[:, h:h + 1], (rows, GDN_D))
        gc = jnp.broadcast_to(gc_all[:, nh + h:nh + h + 1], (rows, GDN_D))
        gc3 = gc.reshape(nc, CHUNK, GDN_D)
        glast = jnp.broadcast_to(gc3[:, CHUNK - 1:CHUNK, :], (nc, CHUNK, GDN_D))
        eg = jnp.exp(gc)
        kb = k * beta
        three = lambda a: a.reshape(nc, CHUNK, a.shape[-1])
        parts["q"].append(three(q.astype(BF16)))
        parts["k"].append(three(k.astype(BF16)))
        parts["kb"].append(three(kb.astype(BF16)))
        parts["rhs"].append(three(jnp.concatenate([v * beta, kb * eg], axis=1).astype(BF16)))
        parts["qg"].append(three((q * eg).astype(BF16)))
        parts["kg"].append(three(k) * jnp.exp(glast - gc3))
        parts["gcb"].append(gc3[:, :, :CHUNK])
        parts["glast"].append(glast[:, :1, :])
        for c in range(nc):
            parts["gcrow"].append(
                gc_t[nh + h:nh + h + 1, c * CHUNK:(c + 1) * CHUNK].reshape(1, 1, CHUNK))
    st = {name: jnp.concatenate(v, axis=0) for name, v in parts.items()}

    decay = jnp.exp(jnp.where(tril, st["gcb"] - st["gcrow"], -jnp.inf))
    kq = jnp.einsum("gmd,gnd->gmn", jnp.concatenate([st["kb"], st["q"]], axis=1), st["k"],
                    preferred_element_type=F32)
    lmat = jnp.where(strict, kq[:, :CHUNK] * decay, 0.0)
    intra = (kq[:, CHUNK:] * decay).astype(BF16)
    tinv = _unit_lower_inverse(lmat, ri, ci, masks)
    sol = _bmm(tinv.astype(BF16), st["rhs"])

    four = lambda a: a.reshape((nh, nc) + a.shape[1:])
    u4 = four(sol[:, :, :GDN_D])
    w4 = four(sol[:, :, GDN_D:].astype(BF16))
    qg4, kg4, intra4 = four(st["qg"]), four(st["kg"]), four(intra)
    egl4 = four(jnp.exp(st["glast"]))
    zg = _silu(z_ref[...])
    state = state_ref[...]
    for c in range(nc):
        wq = jnp.concatenate([w4[:, c], qg4[:, c]], axis=1)
        ws_qs = _bmm(wq, state.astype(BF16))
        v_new = (u4[:, c] - ws_qs[:, :CHUNK]).astype(BF16)
        o_c = ws_qs[:, CHUNK:] + _bmm(intra4[:, c], v_new)
        kgt = jnp.stack([jnp.transpose(kg4[h, c]) for h in range(nh)], axis=0).astype(BF16)
        state = state * egl4[:, c] + _bmm(kgt, v_new)
        sl = slice(c * CHUNK, (c + 1) * CHUNK)
        for h in range(nh):
            cs = slice(h * GDN_D, (h + 1) * GDN_D)
            o_ref[sl, cs] = _rms(o_c[h], ng_ref[...]) * zg[sl, cs]
    state_ref[...] = state
    carry_ref[...] = qkv_ref[rows - 8:rows, :]


def gdn(proj, conv_w, a_log_p, dt_bias_p, norm_g, *, batch, seq, rows):
    t = batch * seq
    nt = seq // rows
    hw = GDN_HEADS * GDN_D
    kern = functools.partial(_gdn_kernel, rows=rows)
    const = lambda b, i: (0, 0)
    return pl.pallas_call(
        kern,
        grid=(batch, nt),
        in_specs=[pl.BlockSpec((rows, 3 * hw), lambda b, i: (b * nt + i, 0)),
                  pl.BlockSpec((rows, hw), lambda b, i: (b * nt + i, 3)),
                  pl.BlockSpec((rows, LANES), lambda b, i: (b * nt + i, (4 * hw + 256) // LANES)),
                  pl.BlockSpec((CONV_K, 3 * hw), const),
                  pl.BlockSpec((1, LANES), const),
                  pl.BlockSpec((1, LANES), const),
                  pl.BlockSpec((1, GDN_D), const)],
        out_specs=pl.BlockSpec((rows, hw), lambda b, i: (b * nt + i, 0)),
        out_shape=jax.ShapeDtypeStruct((t, hw), F32),
        scratch_shapes=[pltpu.VMEM((GDN_HEADS, GDN_D, GDN_D), F32),
                        pltpu.VMEM((8, 3 * hw), F32)],
        compiler_params=_cparams(("arbitrary", "arbitrary")),
        name="gdn",
    )(proj, proj, proj, conv_w, a_log_p, dt_bias_p, norm_g.reshape(1, GDN_D))


def _swa_kernel(sink_ref, q_ref, cos_ref, sin_ref, kvc_ref, kvp_ref, o_ref, *, qrows):
    t = pl.program_id(1)
    nsub = qrows // SWA_BLOCK
    lane = lax.broadcasted_iota(jnp.int32, (1, LANES), 1)
    lo = lane < SWA_DH
    qi = lax.broadcasted_iota(jnp.int32, (SWA_BLOCK, 2 * SWA_BLOCK), 0)
    ki = lax.broadcasted_iota(jnp.int32, (SWA_BLOCK, 2 * SWA_BLOCK), 1)
    band = (ki > qi) & (ki <= qi + SWA_BLOCK)
    band_first = band & (ki >= jnp.where(t > 0, 0, SWA_BLOCK))
    dn = (((1,), (1,)), ((), ()))
    pairs_per_kv = SWA_HEADS // SWA_KV_HEADS // 2

    kv_all = jnp.concatenate([kvp_ref[...], kvc_ref[...]], axis=0)
    kblk = kv_all[:, :LANES]
    vblk = kv_all[:, LANES:]
    kroll = pltpu.roll(kblk, SWA_DH, axis=1)
    vroll = pltpu.roll(vblk, SWA_DH, axis=1)
    kexp = [(jnp.where(lo, kblk, 0.0).astype(BF16), jnp.where(lo, 0.0, kroll).astype(BF16)),
            (jnp.where(lo, kroll, 0.0).astype(BF16), jnp.where(lo, 0.0, kblk).astype(BF16))]
    vexp = [(jnp.where(lo, vblk, 0.0).astype(BF16), jnp.where(lo, 0.0, vroll).astype(BF16)),
            (jnp.where(lo, vroll, 0.0).astype(BF16), jnp.where(lo, 0.0, vblk).astype(BF16))]

    for j in range(nsub):
        ws = slice(j * SWA_BLOCK, (j + 2) * SWA_BLOCK)
        rs = slice(j * SWA_BLOCK, (j + 1) * SWA_BLOCK)
        mask = band_first if j == 0 else band
        cosp, sinp = cos_ref[rs, :], sin_ref[rs, :]
        groups = [(g, half) for g in range(SWA_KV_HEADS) for half in range(2)]
        pairs = lambda g: [g * pairs_per_kv + i for i in range(pairs_per_kv)]
        qs = [jnp.concatenate(
            [(_rope(q_ref[rs, p * LANES:(p + 1) * LANES], cosp, sinp) * (SWA_DH ** -0.5)).astype(BF16)
             for p in pairs(g)], axis=0) for g in range(SWA_KV_HEADS)]
        scs = [lax.dot_general(qs[g], kexp[g][half][ws], dn, preferred_element_type=F32)
               for g, half in groups]
        es, scales = [], []
        for (g, half), sc in zip(groups, scs):
            e_g, s_g = [], []
            for i, p in enumerate(pairs(g)):
                sink = sink_ref[2 * p + half]
                s_i = jnp.where(mask, sc[i * SWA_BLOCK:(i + 1) * SWA_BLOCK], -jnp.inf)
                m = jnp.maximum(jnp.max(s_i, axis=-1, keepdims=True), sink)
                e = jnp.exp(s_i - m)
                den = jnp.sum(e, axis=-1, keepdims=True) + jnp.exp(sink - m)
                e_g.append(e.astype(BF16))
                s_g.append(jnp.broadcast_to(1.0 / den, (SWA_BLOCK, LANES)))
            es.append(jnp.concatenate(e_g, axis=0))
            scales.append(jnp.concatenate(s_g, axis=0))
        pvs = [jnp.dot(es[n], vexp[g][half][ws], preferred_element_type=F32) * scales[n]
               for n, (g, half) in enumerate(groups)]
        for g in range(SWA_KV_HEADS):
            acc = pvs[2 * g] + pvs[2 * g + 1]
            for i, p in enumerate(pairs(g)):
                o_ref[rs, p * LANES:(p + 1) * LANES] = acc[i * SWA_BLOCK:(i + 1) * SWA_BLOCK]


def swa(proj, kv, sinks, cosp, sinp, *, batch, seq, qrows):
    t = batch * seq
    nt = seq // qrows
    qw = SWA_HEADS * SWA_DH
    nsub = qrows // SWA_BLOCK
    nblk = seq // SWA_BLOCK
    row = lambda b, i: (b * nt + i, 0)
    kern = functools.partial(_swa_kernel, qrows=qrows)
    return pl.pallas_call(
        kern,
        grid=(batch, nt),
        in_specs=[pl.BlockSpec(memory_space=pltpu.SMEM),
                  pl.BlockSpec((qrows, qw), row),
                  pl.BlockSpec((qrows, LANES), row), pl.BlockSpec((qrows, LANES), row),
                  pl.BlockSpec((qrows, 2 * LANES), row),
                  pl.BlockSpec((SWA_BLOCK, 2 * LANES),
                               lambda b, i: (b * nblk + jnp.maximum(i * nsub - 1, 0), 0))],
        out_specs=pl.BlockSpec((qrows, qw), row),
        out_shape=jax.ShapeDtypeStruct((t, qw), F32),
        compiler_params=_cparams(("parallel", "arbitrary")),
        name="swa",
    )(sinks, proj, cosp, sinp, kv, kv)


def _mix_ffn_kernel(x_ref, mix_ref, mq_ref, mkv_ref, wo_ref, g_ref, wg_ref, wu_ref, wd_ref, gf_ref,
                    o_ref, acc_ref, *, ff_tile, final):
    lane = lax.broadcasted_iota(jnp.int32, (1, LANES), 1)
    lo = lane < MEM_DH
    mw = MEM_HEADS * MEM_DH
    dn = (((1,), (1,)), ((), ()))
    outs = [mix_ref[...].astype(BF16)]
    for p in range(MEM_HEADS // 2):
        qp = (mq_ref[:, p * LANES:(p + 1) * LANES] * (MEM_DH ** -0.5)).astype(BF16)
        kp = mkv_ref[:, p * LANES:(p + 1) * LANES]
        vp = mkv_ref[:, mw + p * LANES:mw + (p + 1) * LANES]
        acc = None
        for half in range(2):
            sel = lo if half == 0 else jnp.logical_not(lo)
            kh = jnp.where(sel, kp, 0.0).astype(BF16)
            vh = jnp.where(sel, vp, 0.0).astype(BF16)
            sc = lax.dot_general(qp, kh, dn, preferred_element_type=F32)
            m = jnp.max(sc, axis=-1, keepdims=True)
            e = jnp.exp(sc - m)
            scale = jnp.broadcast_to(1.0 / jnp.sum(e, axis=-1, keepdims=True), (e.shape[0], LANES))
            pv = jnp.dot(e.astype(BF16), vh, preferred_element_type=F32) * scale
            acc = pv if acc is None else acc + pv
        outs.append(acc.astype(BF16))
    cat = jnp.concatenate(outs, axis=1)
    x = x_ref[...] + jnp.dot(cat, wo_ref[...], preferred_element_type=F32)
    h = _rms(x, g_ref[...]).astype(BF16)
    d_ff = wg_ref.shape[1]
    acc_ref[...] = x
    for j in range(d_ff // ff_tile):
        cs = slice(j * ff_tile, (j + 1) * ff_tile)
        gate = jnp.dot(h, wg_ref[:, cs], preferred_element_type=F32)
        up = jnp.dot(h, wu_ref[:, cs], preferred_element_type=F32)
        act = (_silu(gate) * up).astype(BF16)
        acc_ref[...] += jnp.dot(act, wd_ref[cs, :], preferred_element_type=F32)
    y = acc_ref[...]
    o_ref[...] = _rms(y, gf_ref[...]) if final else y


def mix_ffn(x, mix, proj, mq_block, mkv, w_out, g, wgu, wd, g_final, *, batch, seq, tm, ff_tile, final):
    t, d = x.shape
    nt = seq // tm
    mixw = mix.shape[1]
    mw = MEM_HEADS * MEM_DH
    mlen = mkv.shape[0] // batch
    d_ff = wd.shape[0]
    row = lambda b, i: (b * nt + i, 0)
    const = lambda b, i: (0, 0)
    single = pl.Buffered(1)
    kern = functools.partial(_mix_ffn_kernel, ff_tile=ff_tile, final=final)
    return pl.pallas_call(
        kern,
        grid=(batch, nt),
        in_specs=[pl.BlockSpec((tm, d), row),
                  pl.BlockSpec((tm, mixw), row),
                  pl.BlockSpec((tm, mw), lambda b, i: (b * nt + i, mq_block)),
                  pl.BlockSpec((mlen, 2 * mw), lambda b, i: (b, 0)),
                  pl.BlockSpec((mixw + mw, d), const, pipeline_mode=single),
                  pl.BlockSpec((1, d), const),
                  pl.BlockSpec((d, d_ff), const, pipeline_mode=single),
                  pl.BlockSpec((d, d_ff), lambda b, i: (0, 1), pipeline_mode=single),
                  pl.BlockSpec((d_ff, d), const, pipeline_mode=single),
                  pl.BlockSpec((1, d), const)],
        out_specs=pl.BlockSpec((tm, d), row),
        out_shape=jax.ShapeDtypeStruct((t, d), F32),
        scratch_shapes=[pltpu.VMEM((tm, d), F32)],
        compiler_params=_cparams(("parallel", "parallel")),
        name="mix_ffn",
    )(x, mix, proj, mkv, w_out, g.reshape(1, d), wgu, wgu, wd, g_final.reshape(1, d))


def _rope_lane_tables(positions):
    half = ROT_DIM // 2
    inv = ROPE_THETA ** (-jnp.arange(0, ROT_DIM, 2, dtype=F32) / ROT_DIM)
    lane = jnp.arange(LANES) % SWA_DH
    ang = positions.astype(F32).reshape(-1, 1) * inv[lane % half][None, :]
    cosp = jnp.where(lane < ROT_DIM, jnp.cos(ang), 1.0)
    sin = jnp.sin(ang)
    sinp = jnp.where(lane < half, -sin, jnp.where(lane < ROT_DIM, sin, 0.0))
    return cosp, sinp


def _pad_lanes(v, offset):
    out = jnp.zeros((1, LANES), F32)
    return lax.dynamic_update_slice(out, v.reshape(1, -1).astype(F32), (0, offset))


def kernel(x, mem, positions, ln_mix, ln_ffn, ln_mem, w_mem_kv, w_out, w_gate_up, w_down,
           gdn_w_in, gdn_conv, gdn_A_log, gdn_dt_bias, gdn_norm,
           swa_w_q, swa_sinks, ln_kv, w_kv, ln_final):
    batch, seq, d = x.shape
    depth = ln_mix.shape[0]
    n_a = gdn_w_in.shape[0]
    t = batch * seq
    mlen = mem.shape[1]
    hw = GDN_HEADS * GDN_D
    mw = MEM_HEADS * MEM_DH

    cosp, sinp = _rope_lane_tables(positions)
    xs = x.reshape(t, d)
    mem2 = mem.reshape(batch * mlen, d)
    kv = None
    for layer in range(depth):
        mkv = norm_matmul(mem2, ln_mem, w_mem_kv[layer].astype(BF16), tm=batch * mlen)
        if layer < n_a:
            a = layer
            w_in = gdn_w_in[a]
            o2 = 4 * hw
            w_r = jnp.concatenate(
                [w_in[:, :o2], w_in[:, o2 + 2 * GDN_HEADS:], w_in[:, o2:o2 + 2 * GDN_HEADS],
                 jnp.zeros((d, LANES - 2 * GDN_HEADS), F32)], axis=1).astype(BF16)
            proj = norm_matmul(xs, ln_mix[layer], w_r, tm=512)
            mix = gdn(proj, gdn_conv[a], _pad_lanes(gdn_A_log[a], GDN_HEADS),
                      _pad_lanes(gdn_dt_bias[a], GDN_HEADS), gdn_norm[a],
                      batch=batch, seq=seq, rows=256)
            mq_block = o2 // mw
        else:
            bl = layer - n_a
            proj = norm_matmul(xs, ln_mix[layer], swa_w_q[bl].astype(BF16), tm=512)
            mix = swa(proj, kv, swa_sinks[bl], cosp, sinp, batch=batch, seq=seq, qrows=512)
            mq_block = (SWA_HEADS * SWA_DH) // mw
        xs = mix_ffn(xs, mix, proj, mq_block, mkv, w_out[layer].astype(BF16), ln_ffn[layer],
                     w_gate_up[layer].astype(BF16), w_down[layer].astype(BF16), ln_final,
                     batch=batch, seq=seq, tm=512, ff_tile=256, final=(layer == depth - 1))
        if layer == n_a - 1:
            kv = kv_proj(xs, ln_kv, w_kv.astype(BF16), cosp, sinp, tm=512)
    return xs.reshape(batch, seq, d)
```

```python
import functools

import jax
import jax.numpy as jnp
from jax import lax
from jax.experimental import pallas as pl
from jax.experimental.pallas import tpu as pltpu

F32 = jnp.float32
BF16 = jnp.bfloat16

EPS = 1e-6
LANES = 128
CHUNK = 64
CONV_K = 4
GDN_HEADS = 6
GDN_D = 128
SWA_HEADS = 12
SWA_KV_HEADS = 2
SWA_DH = 64
SWA_BLOCK = 128
ROT_DIM = 16
ROPE_THETA = 500000.0
MEM_HEADS = 4
MEM_DH = 64
VMEM_LIMIT = 56 * 1024 * 1024


def _cparams(sem):
    return pltpu.CompilerParams(dimension_semantics=sem, vmem_limit_bytes=VMEM_LIMIT)


def _rms(x, g):
    ms = jnp.mean(x * x, axis=-1, keepdims=True)
    return x * lax.rsqrt(ms + EPS) * g


def _sigmoid(x):
    return 0.5 + 0.5 * jnp.tanh(0.5 * x)


def _silu(x):
    h = 0.5 * x
    return h + h * jnp.tanh(h)


def _rope(x, cosp, sinp):
    half = ROT_DIM // 2
    lane = lax.broadcasted_iota(jnp.int32, (1, LANES), 1) & (SWA_DH - 1)
    partner = jnp.where(lane < half, pltpu.roll(x, LANES - half, axis=1), pltpu.roll(x, half, axis=1))
    return x * cosp + partner * sinp


def _norm_matmul_kernel(x_ref, g_ref, w_ref, o_ref):
    h = _rms(x_ref[...], g_ref[...]).astype(BF16)
    o_ref[...] = jnp.dot(h, w_ref[...], preferred_element_type=F32)


def norm_matmul(x, g, w, *, tm):
    t, d = x.shape
    n = w.shape[1]
    return pl.pallas_call(
        _norm_matmul_kernel,
        grid=(t // tm,),
        in_specs=[pl.BlockSpec((tm, d), lambda i: (i, 0)),
                  pl.BlockSpec((1, d), lambda i: (0, 0)),
                  pl.BlockSpec((d, n), lambda i: (0, 0))],
        out_specs=pl.BlockSpec((tm, n), lambda i: (i, 0)),
        out_shape=jax.ShapeDtypeStruct((t, n), F32),
        compiler_params=_cparams(("parallel",)),
        name="norm_matmul",
    )(x, g.reshape(1, d), w)


def _kv_proj_kernel(x_ref, g_ref, w_ref, cos_ref, sin_ref, o_ref):
    h = _rms(x_ref[...], g_ref[...]).astype(BF16)
    kv = jnp.dot(h, w_ref[...], preferred_element_type=F32)
    o_ref[:, :LANES] = _rope(kv[:, :LANES], cos_ref[...], sin_ref[...])
    o_ref[:, LANES:] = kv[:, LANES:]


def kv_proj(x, g, w, cosp, sinp, *, tm):
    t, d = x.shape
    n = w.shape[1]
    row = lambda i: (i, 0)
    const = lambda i: (0, 0)
    return pl.pallas_call(
        _kv_proj_kernel,
        grid=(t // tm,),
        in_specs=[pl.BlockSpec((tm, d), row), pl.BlockSpec((1, d), const), pl.BlockSpec((d, n), const),
                  pl.BlockSpec((tm, LANES), row), pl.BlockSpec((tm, LANES), row)],
        out_specs=pl.BlockSpec((tm, n), row),
        out_shape=jax.ShapeDtypeStruct((t, n), F32),
        compiler_params=_cparams(("parallel",)),
        name="kv_proj",
    )(x, g.reshape(1, d), w, cosp, sinp)


def _level_masks():
    ri = lax.broadcasted_iota(jnp.int32, (CHUNK, CHUNK), 0)
    ci = lax.broadcasted_iota(jnp.int32, (CHUNK, CHUNK), 1)
    masks = []
    for l in range(1, 7):
        same = (ri >> l) == (ci >> l)
        diff_half = (ri >> (l - 1)) != (ci >> (l - 1))
        masks.append((same & diff_half & (ri > ci)).astype(F32))
    return ri, ci, masks


def _bmm(a, b):
    return jnp.einsum("gmk,gkn->gmn", a, b, preferred_element_type=F32)


def _unit_lower_inverse(lmat, ri, ci, masks):
    eye = (ri == ci).astype(F32)
    p = eye - lmat * masks[0]
    for l in range(1, 6):
        e = (lmat * masks[l]).astype(BF16)
        pb = p.astype(BF16)
        pe = _bmm(pb, e)
        p = p - _bmm(pe.astype(BF16), pb)
    return p


def _gdn_kernel(qkv_ref, z_ref, ba_ref, cw_ref, alog_ref, dtb_ref, ng_ref, o_ref,
                state_ref, carry_ref, *, rows):
    t = pl.program_id(1)
    nc = rows // CHUNK
    nh = GDN_HEADS
    hw = nh * GDN_D

    @pl.when(t == 0)
    def _():
        state_ref[...] = jnp.zeros_like(state_ref)
        carry_ref[...] = jnp.zeros_like(carry_ref)

    ri, ci, masks = _level_masks()
    tril = ri >= ci
    strict = (ri > ci).astype(F32)

    ba = ba_ref[...]
    beta_all = _sigmoid(ba)
    sp_in = ba + dtb_ref[...]
    softplus = jnp.maximum(sp_in, 0.0) + jnp.log(1.0 + jnp.exp(-jnp.abs(sp_in)))
    g_all = -jnp.exp(alog_ref[...]) * softplus
    ti = lax.broadcasted_iota(jnp.int32, (rows, rows), 0)
    tj = lax.broadcasted_iota(jnp.int32, (rows, rows), 1)
    tri = (((ti // CHUNK) == (tj // CHUNK)) & (tj <= ti)).astype(BF16)
    g_hi = g_all.astype(BF16)
    rem = g_all - g_hi.astype(F32)
    g_mid = rem.astype(BF16)
    g_lo = (rem - g_mid.astype(F32)).astype(BF16)
    gsum = jnp.dot(tri, jnp.concatenate([g_hi, g_mid, g_lo], axis=1), preferred_element_type=F32)
    gc_all = gsum[:, :LANES] + gsum[:, LANES:2 * LANES] + gsum[:, 2 * LANES:]
    gc_t = jnp.transpose(gc_all)

    row8 = lax.broadcasted_iota(jnp.int32, (8, GDN_D), 0)

    def conv_silu(col):
        x = qkv_ref[:, col:col + GDN_D]
        c8 = carry_ref[:, col:col + GDN_D]
        acc = x * cw_ref[CONV_K - 1:CONV_K, col:col + GDN_D]
        for k in range(1, CONV_K):
            xr = pltpu.roll(x, k, axis=0)
            cr = pltpu.roll(c8, k, axis=0)
            head = jnp.where(row8 < k, cr, xr[:8])
            sh = jnp.concatenate([head, xr[8:]], axis=0)
            acc = acc + sh * cw_ref[CONV_K - 1 - k:CONV_K - k, col:col + GDN_D]
        return _silu(acc)

    parts = {name: [] for name in ("q", "k", "kb", "rhs", "qg", "kg", "gcb", "gcrow", "glast")}
    for h in range(nh):
        q = conv_silu(h * GDN_D)
        k = conv_silu(hw + h * GDN_D)
        v = conv_silu(2 * hw + h * GDN_D)
        q = q * lax.rsqrt(jnp.sum(q * q, axis=-1, keepdims=True) + EPS) * (GDN_D ** -0.5)
        k = k * lax.rsqrt(jnp.sum(k * k, axis=-1, keepdims=True) + EPS)
        beta = jnp.broadcast_to(beta_all[:, h:h + 1], (rows, GDN_D))
        gc = jnp.broadcast_to(gc_all[:, nh + h:nh + h + 1], (rows, GDN_D))
        gc3 = gc.reshape(nc, CHUNK, GDN_D)
        glast = jnp.broadcast_to(gc3[:, CHUNK - 1:CHUNK, :], (nc, CHUNK, GDN_D))
        eg = jnp.exp(gc)
        kb = k * beta
        three = lambda a: a.reshape(nc, CHUNK, a.shape[-1])
        parts["q"].append(three(q.astype(BF16)))
        parts["k"].append(three(k.astype(BF16)))
        parts["kb"].append(three(kb.astype(BF16)))
        parts["rhs"].append(three(jnp.concatenate([v * beta, kb * eg], axis=1).astype(BF16)))
        parts["qg"].append(three((q * eg).astype(BF16)))
        parts["kg"].append(three(k) * jnp.exp(glast - gc3))
        parts["gcb"].append(gc3[:, :, :CHUNK])
        parts["glast"].append(glast[:, :1, :])
        for c in range(nc):
            parts["gcrow"].append(
                gc_t[nh + h:nh + h + 1, c * CHUNK:(c + 1) * CHUNK].reshape(1, 1, CHUNK))
    st = {name: jnp.concatenate(v, axis=0) for name, v in parts.items()}

    decay = jnp.exp(jnp.where(tril, st["gcb"] - st["gcrow"], -jnp.inf))
    kq = jnp.einsum("gmd,gnd->gmn", jnp.concatenate([st["kb"], st["q"]], axis=1), st["k"],
                    preferred_element_type=F32)
    lmat = kq[:, :CHUNK] * (decay * strict)
    intra = (kq[:, CHUNK:] * decay).astype(BF16)
    tinv = _unit_lower_inverse(lmat, ri, ci, masks)
    sol = _bmm(tinv.astype(BF16), st["rhs"])

    four = lambda a: a.reshape((nh, nc) + a.shape[1:])
    u4 = four(sol[:, :, :GDN_D])
    w4 = four(sol[:, :, GDN_D:].astype(BF16))
    qg4, kg4, intra4 = four(st["qg"]), four(st["kg"]), four(intra)
    egl4 = four(jnp.exp(st["glast"]))
    zg = _silu(z_ref[...])
    state = state_ref[...]
    for c in range(nc):
        wq = jnp.concatenate([w4[:, c], qg4[:, c]], axis=1)
        ws_qs = _bmm(wq, state.astype(BF16))
        v_new = (u4[:, c] - ws_qs[:, :CHUNK]).astype(BF16)
        o_c = ws_qs[:, CHUNK:] + _bmm(intra4[:, c], v_new)
        kgt = jnp.stack([jnp.transpose(kg4[h, c]) for h in range(nh)], axis=0).astype(BF16)
        state = state * egl4[:, c] + _bmm(kgt, v_new)
        sl = slice(c * CHUNK, (c + 1) * CHUNK)
        for h in range(nh):
            cs = slice(h * GDN_D, (h + 1) * GDN_D)
            o_ref[sl, cs] = _rms(o_c[h], ng_ref[...]) * zg[sl, cs]
    state_ref[...] = state
    carry_ref[...] = qkv_ref[rows - 8:rows, :]


def gdn(proj, conv_w, a_log_p, dt_bias_p, norm_g, *, batch, seq, rows):
    t = batch * seq
    nt = seq // rows
    hw = GDN_HEADS * GDN_D
    kern = functools.partial(_gdn_kernel, rows=rows)
    const = lambda b, i: (0, 0)
    return pl.pallas_call(
        kern,
        grid=(batch, nt),
        in_specs=[pl.BlockSpec((rows, 3 * hw), lambda b, i: (b * nt + i, 0)),
                  pl.BlockSpec((rows, hw), lambda b, i: (b * nt + i, 3)),
                  pl.BlockSpec((rows, LANES), lambda b, i: (b * nt + i, (4 * hw + 256) // LANES)),
                  pl.BlockSpec((CONV_K, 3 * hw), const),
                  pl.BlockSpec((1, LANES), const),
                  pl.BlockSpec((1, LANES), const),
                  pl.BlockSpec((1, GDN_D), const)],
        out_specs=pl.BlockSpec((rows, hw), lambda b, i: (b * nt + i, 0)),
        out_shape=jax.ShapeDtypeStruct((t, hw), F32),
        scratch_shapes=[pltpu.VMEM((GDN_HEADS, GDN_D, GDN_D), F32),
                        pltpu.VMEM((8, 3 * hw), F32)],
        compiler_params=_cparams(("arbitrary", "arbitrary")),
        name="gdn",
    )(proj, proj, proj, conv_w, a_log_p, dt_bias_p, norm_g.reshape(1, GDN_D))


def _swa_kernel(sink_ref, q_ref, cos_ref, sin_ref, kvc_ref, kvp_ref, o_ref, *, qrows):
    t = pl.program_id(1)
    nsub = qrows // SWA_BLOCK
    lane = lax.broadcasted_iota(jnp.int32, (1, LANES), 1)
    lo = lane < SWA_DH
    qi = lax.broadcasted_iota(jnp.int32, (SWA_BLOCK, 2 * SWA_BLOCK), 0)
    ki = lax.broadcasted_iota(jnp.int32, (SWA_BLOCK, 2 * SWA_BLOCK), 1)
    band = (ki > qi) & (ki <= qi + SWA_BLOCK)
    band_first = band & (ki >= jnp.where(t > 0, 0, SWA_BLOCK))
    dn = (((1,), (1,)), ((), ()))
    pairs_per_kv = SWA_HEADS // SWA_KV_HEADS // 2
    ones = jnp.ones((2 * SWA_BLOCK, LANES), BF16)

    kv_all = jnp.concatenate([kvp_ref[...], kvc_ref[...]], axis=0)
    kblk = kv_all[:, :LANES]
    vblk = kv_all[:, LANES:]
    kroll = pltpu.roll(kblk, SWA_DH, axis=1)
    vroll = pltpu.roll(vblk, SWA_DH, axis=1)
    kexp = [(jnp.where(lo, kblk, 0.0).astype(BF16), jnp.where(lo, 0.0, kroll).astype(BF16)),
            (jnp.where(lo, kroll, 0.0).astype(BF16), jnp.where(lo, 0.0, kblk).astype(BF16))]
    vexp = [(jnp.where(lo, vblk, 0.0).astype(BF16), jnp.where(lo, 0.0, vroll).astype(BF16)),
            (jnp.where(lo, vroll, 0.0).astype(BF16), jnp.where(lo, 0.0, vblk).astype(BF16))]

    for j in range(nsub):
        ws = slice(j * SWA_BLOCK, (j + 2) * SWA_BLOCK)
        rs = slice(j * SWA_BLOCK, (j + 1) * SWA_BLOCK)
        mask = band_first if j == 0 else band
        cosp, sinp = cos_ref[rs, :], sin_ref[rs, :]
        groups = [(g, half) for g in range(SWA_KV_HEADS) for half in range(2)]
        pairs = lambda g: [g * pairs_per_kv + i for i in range(pairs_per_kv)]
        qs = [jnp.concatenate(
            [(_rope(q_ref[rs, p * LANES:(p + 1) * LANES], cosp, sinp) * (SWA_DH ** -0.5)).astype(BF16)
             for p in pairs(g)], axis=0) for g in range(SWA_KV_HEADS)]
        scs = [lax.dot_general(qs[g], kexp[g][half][ws], dn, preferred_element_type=F32)
               for g, half in groups]
        es, sink_terms = [], []
        for (g, half), sc in zip(groups, scs):
            e_g, s_g = [], []
            for i, p in enumerate(pairs(g)):
                sink = sink_ref[2 * p + half]
                s_i = jnp.where(mask, sc[i * SWA_BLOCK:(i + 1) * SWA_BLOCK], -jnp.inf)
                m = jnp.maximum(jnp.max(s_i, axis=-1, keepdims=True), sink)
                e_g.append(jnp.exp(s_i - m).astype(BF16))
                s_g.append(jnp.broadcast_to(jnp.exp(sink - m), (SWA_BLOCK, LANES)))
            es.append(jnp.concatenate(e_g, axis=0))
            sink_terms.append(jnp.concatenate(s_g, axis=0))
        pvs = [jnp.dot(es[n], vexp[g][half][ws], preferred_element_type=F32)
               / (jnp.dot(es[n], ones, preferred_element_type=F32) + sink_terms[n])
               for n, (g, half) in enumerate(groups)]
        for g in range(SWA_KV_HEADS):
            acc = pvs[2 * g] + pvs[2 * g + 1]
            for i, p in enumerate(pairs(g)):
                o_ref[rs, p * LANES:(p + 1) * LANES] = acc[i * SWA_BLOCK:(i + 1) * SWA_BLOCK]


def swa(proj, kv, sinks, cosp, sinp, *, batch, seq, qrows):
    t = batch * seq
    nt = seq // qrows
    qw = SWA_HEADS * SWA_DH
    nsub = qrows // SWA_BLOCK
    nblk = seq // SWA_BLOCK
    row = lambda b, i: (b * nt + i, 0)
    kern = functools.partial(_swa_kernel, qrows=qrows)
    return pl.pallas_call(
        kern,
        grid=(batch, nt),
        in_specs=[pl.BlockSpec(memory_space=pltpu.SMEM),
                  pl.BlockSpec((qrows, qw), row),
                  pl.BlockSpec((qrows, LANES), row), pl.BlockSpec((qrows, LANES), row),
                  pl.BlockSpec((qrows, 2 * LANES), row),
                  pl.BlockSpec((SWA_BLOCK, 2 * LANES),
                               lambda b, i: (b * nblk + jnp.maximum(i * nsub - 1, 0), 0))],
        out_specs=pl.BlockSpec((qrows, qw), row),
        out_shape=jax.ShapeDtypeStruct((t, qw), F32),
        compiler_params=_cparams(("parallel", "arbitrary")),
        name="swa",
    )(sinks, proj, cosp, sinp, kv, kv)


def _mix_ffn_kernel(x_ref, mix_ref, mq_ref, mkv_ref, wo_ref, g_ref, wg_ref, wu_ref, wd_ref, gf_ref,
                    o_ref, acc_ref, *, ff_tile, final):
    lane = lax.broadcasted_iota(jnp.int32, (1, LANES), 1)
    lo = lane < MEM_DH
    mw = MEM_HEADS * MEM_DH
    dn = (((1,), (1,)), ((), ()))
    outs = [mix_ref[...].astype(BF16)]
    for p in range(MEM_HEADS // 2):
        qp = (mq_ref[:, p * LANES:(p + 1) * LANES] * (MEM_DH ** -0.5)).astype(BF16)
        kp = mkv_ref[:, p * LANES:(p + 1) * LANES]
        vp = mkv_ref[:, mw + p * LANES:mw + (p + 1) * LANES]
        acc = None
        for half in range(2):
            sel = lo if half == 0 else jnp.logical_not(lo)
            kh = jnp.where(sel, kp, 0.0).astype(BF16)
            vh = jnp.where(sel, vp, 0.0).astype(BF16)
            sc = lax.dot_general(qp, kh, dn, preferred_element_type=F32)
            m = jnp.max(sc, axis=-1, keepdims=True)
            e = jnp.exp(sc - m)
            scale = jnp.broadcast_to(1.0 / jnp.sum(e, axis=-1, keepdims=True), (e.shape[0], LANES))
            pv = jnp.dot(e.astype(BF16), vh, preferred_element_type=F32) * scale
            acc = pv if acc is None else acc + pv
        outs.append(acc.astype(BF16))
    cat = jnp.concatenate(outs, axis=1)
    x = x_ref[...] + jnp.dot(cat, wo_ref[...], preferred_element_type=F32)
    h = _rms(x, g_ref[...]).astype(BF16)
    d_ff = wg_ref.shape[1]
    acc_ref[...] = x
    for j in range(d_ff // ff_tile):
        cs = slice(j * ff_tile, (j + 1) * ff_tile)
        gate = jnp.dot(h, wg_ref[:, cs], preferred_element_type=F32)
        up = jnp.dot(h, wu_ref[:, cs], preferred_element_type=F32)
        act = (_silu(gate) * up).astype(BF16)
        acc_ref[...] += jnp.dot(act, wd_ref[cs, :], preferred_element_type=F32)
    y = acc_ref[...]
    o_ref[...] = _rms(y, gf_ref[...]) if final else y


def mix_ffn(x, mix, proj, mq_block, mkv, w_out, g, wgu, wd, g_final, *, batch, seq, tm, ff_tile, final):
    t, d = x.shape
    nt = seq // tm
    mixw = mix.shape[1]
    mw = MEM_HEADS * MEM_DH
    mlen = mkv.shape[0] // batch
    d_ff = wd.shape[0]
    row = lambda b, i: (b * nt + i, 0)
    const = lambda b, i: (0, 0)
    single = pl.Buffered(1)
    kern = functools.partial(_mix_ffn_kernel, ff_tile=ff_tile, final=final)
    return pl.pallas_call(
        kern,
        grid=(batch, nt),
        in_specs=[pl.BlockSpec((tm, d), row),
                  pl.BlockSpec((tm, mixw), row),
                  pl.BlockSpec((tm, mw), lambda b, i: (b * nt + i, mq_block)),
                  pl.BlockSpec((mlen, 2 * mw), lambda b, i: (b, 0)),
                  pl.BlockSpec((mixw + mw, d), const, pipeline_mode=single),
                  pl.BlockSpec((1, d), const),
                  pl.BlockSpec((d, d_ff), const, pipeline_mode=single),
                  pl.BlockSpec((d, d_ff), lambda b, i: (0, 1), pipeline_mode=single),
                  pl.BlockSpec((d_ff, d), const, pipeline_mode=single),
                  pl.BlockSpec((1, d), const)],
        out_specs=pl.BlockSpec((tm, d), row),
        out_shape=jax.ShapeDtypeStruct((t, d), F32),
        scratch_shapes=[pltpu.VMEM((tm, d), F32)],
        compiler_params=_cparams(("parallel", "parallel")),
        name="mix_ffn",
    )(x, mix, proj, mkv, w_out, g.reshape(1, d), wgu, wgu, wd, g_final.reshape(1, d))


def _rope_table_kernel(pos_ref, inv_ref, cos_ref, sin_ref):
    half = ROT_DIM // 2
    lane = lax.broadcasted_iota(jnp.int32, (1, LANES), 1) & (SWA_DH - 1)
    for r in range(pos_ref.shape[0]):
        prow = pos_ref[r:r + 1, :].astype(F32)
        pcol = jnp.transpose(jnp.broadcast_to(prow, (LANES, LANES)))
        ang = pcol * inv_ref[...]
        sin = jnp.sin(ang)
        rs = slice(r * LANES, (r + 1) * LANES)
        cos_ref[rs, :] = jnp.where(lane < ROT_DIM, jnp.cos(ang), 1.0)
        sin_ref[rs, :] = jnp.where(lane < half, -sin, jnp.where(lane < ROT_DIM, sin, 0.0))


def rope_tables(positions):
    half = ROT_DIM // 2
    t = positions.size
    rows = 8
    inv = ROPE_THETA ** (-jnp.arange(0, ROT_DIM, 2, dtype=F32) / ROT_DIM)
    inv_lane = inv[(jnp.arange(LANES) % SWA_DH) % half].reshape(1, LANES)
    out = jax.ShapeDtypeStruct((t, LANES), F32)
    return pl.pallas_call(
        _rope_table_kernel,
        grid=(t // (rows * LANES),),
        in_specs=[pl.BlockSpec((rows, LANES), lambda i: (i, 0)), pl.BlockSpec((1, LANES), lambda i: (0, 0))],
        out_specs=[pl.BlockSpec((rows * LANES, LANES), lambda i: (i, 0))] * 2,
        out_shape=[out, out],
        compiler_params=_cparams(("parallel",)),
        name="rope_tables",
    )(positions.reshape(t // LANES, LANES), inv_lane)


def _pad_lanes(v, offset):
    out = jnp.zeros((1, LANES), F32)
    return lax.dynamic_update_slice(out, v.reshape(1, -1).astype(F32), (0, offset))


def kernel(x, mem, positions, ln_mix, ln_ffn, ln_mem, w_mem_kv, w_out, w_gate_up, w_down,
           gdn_w_in, gdn_conv, gdn_A_log, gdn_dt_bias, gdn_norm,
           swa_w_q, swa_sinks, ln_kv, w_kv, ln_final):
    batch, seq, d = x.shape
    depth = ln_mix.shape[0]
    n_a = gdn_w_in.shape[0]
    t = batch * seq
    mlen = mem.shape[1]
    hw = GDN_HEADS * GDN_D
    mw = MEM_HEADS * MEM_DH

    cosp, sinp = rope_tables(positions)
    xs = x.reshape(t, d)
    mem2 = mem.reshape(batch * mlen, d)
    kv = None
    for layer in range(depth):
        mkv = norm_matmul(mem2, ln_mem, w_mem_kv[layer].astype(BF16), tm=batch * mlen)
        if layer < n_a:
            a = layer
            w_in = gdn_w_in[a]
            o2 = 4 * hw
            w_r = jnp.concatenate(
                [w_in[:, :o2], w_in[:, o2 + 2 * GDN_HEADS:], w_in[:, o2:o2 + 2 * GDN_HEADS],
                 jnp.zeros((d, LANES - 2 * GDN_HEADS), F32)], axis=1).astype(BF16)
            proj = norm_matmul(xs, ln_mix[layer], w_r, tm=512)
            mix = gdn(proj, gdn_conv[a], _pad_lanes(gdn_A_log[a], GDN_HEADS),
                      _pad_lanes(gdn_dt_bias[a], GDN_HEADS), gdn_norm[a],
                      batch=batch, seq=seq, rows=256)
            mq_block = o2 // mw
        else:
            bl = layer - n_a
            proj = norm_matmul(xs, ln_mix[layer], swa_w_q[bl].astype(BF16), tm=512)
            mix = swa(proj, kv, swa_sinks[bl], cosp, sinp, batch=batch, seq=seq, qrows=512)
            mq_block = (SWA_HEADS * SWA_DH) // mw
        xs = mix_ffn(xs, mix, proj, mq_block, mkv, w_out[layer].astype(BF16), ln_ffn[layer],
                     w_gate_up[layer].astype(BF16), w_down[layer].astype(BF16), ln_final,
                     batch=batch, seq=seq, tm=512, ff_tile=256, final=(layer == depth - 1))
        if layer == n_a - 1:
            kv = kv_proj(xs, ln_kv, w_kv.astype(BF16), cosp, sinp, tm=512)
    return xs.reshape(batch, seq, d)
```

```python
import functools

import jax
import jax.numpy as jnp
from jax import lax
from jax.experimental import pallas as pl
from jax.experimental.pallas import tpu as pltpu

F32 = jnp.float32
BF16 = jnp.bfloat16

EPS = 1e-6
LANES = 128
CHUNK = 64
CONV_K = 4
GDN_HEADS = 6
GDN_D = 128
SWA_HEADS = 12
SWA_KV_HEADS = 2
SWA_DH = 64
SWA_BLOCK = 128
ROT_DIM = 16
ROPE_THETA = 500000.0
MEM_HEADS = 4
MEM_DH = 64
VMEM_LIMIT = 56 * 1024 * 1024


def _cparams(sem):
    return pltpu.CompilerParams(dimension_semantics=sem, vmem_limit_bytes=VMEM_LIMIT)


def _rms(x, g):
    ms = jnp.mean(x * x, axis=-1, keepdims=True)
    return x * lax.rsqrt(ms + EPS) * g


def _sigmoid(x):
    return 0.5 + 0.5 * jnp.tanh(0.5 * x)


def _silu_of_half(h):
    return h + h * jnp.tanh(h)


def _silu(x):
    return _silu_of_half(0.5 * x)


def _rope(x, cosp, sinp):
    half = ROT_DIM // 2
    lane = lax.broadcasted_iota(jnp.int32, (1, LANES), 1) & (SWA_DH - 1)
    partner = jnp.where(lane < half, pltpu.roll(x, LANES - half, axis=1), pltpu.roll(x, half, axis=1))
    return x * cosp + partner * sinp


def _norm_matmul_kernel(x_ref, g_ref, w_ref, o_ref):
    h = _rms(x_ref[...], g_ref[...]).astype(BF16)
    o_ref[...] = jnp.dot(h, w_ref[...], preferred_element_type=F32)


def norm_matmul(x, g, w, *, tm, layer=None):
    t, d = x.shape
    n = w.shape[-1]
    if layer is None:
        w_spec = pl.BlockSpec((d, n), lambda i: (0, 0))
    else:
        w_spec = pl.BlockSpec((None, d, n), lambda i: (layer, 0, 0))
    return pl.pallas_call(
        _norm_matmul_kernel,
        grid=(t // tm,),
        in_specs=[pl.BlockSpec((tm, d), lambda i: (i, 0)), pl.BlockSpec((1, d), lambda i: (0, 0)), w_spec],
        out_specs=pl.BlockSpec((tm, n), lambda i: (i, 0)),
        out_shape=jax.ShapeDtypeStruct((t, n), F32),
        compiler_params=_cparams(("parallel",)),
        name="norm_matmul",
    )(x, g.reshape(1, d), w)


def _gdn_weight_prep_kernel(w_ref, o_ref, *, o2, n_gate):
    w = w_ref[...]
    n_in = w.shape[1]
    mq = n_in - o2 - n_gate
    zc = 3 * o2 // 4
    o_ref[:, :zc] = w[:, :zc].astype(BF16)
    o_ref[:, zc:o2] = (0.5 * w[:, zc:o2]).astype(BF16)
    o_ref[:, o2:o2 + mq] = w[:, o2 + n_gate:].astype(BF16)
    gates = jnp.concatenate([w[:, o2:o2 + n_gate], jnp.zeros((w.shape[0], LANES - n_gate), F32)], axis=1)
    o_ref[:, o2 + mq:] = gates.astype(BF16)


def gdn_weight_prep(w_in_all, layer, *, o2, n_gate, tr):
    _, d, n_in = w_in_all.shape
    n_out = n_in - n_gate + LANES
    kern = functools.partial(_gdn_weight_prep_kernel, o2=o2, n_gate=n_gate)
    return pl.pallas_call(
        kern,
        grid=(d // tr,),
        in_specs=[pl.BlockSpec((None, tr, n_in), lambda i: (layer, i, 0))],
        out_specs=pl.BlockSpec((tr, n_out), lambda i: (i, 0)),
        out_shape=jax.ShapeDtypeStruct((d, n_out), BF16),
        compiler_params=_cparams(("parallel",)),
        name="gdn_weight_prep",
    )(w_in_all)


def _kv_proj_kernel(x_ref, g_ref, w_ref, cos_ref, sin_ref, o_ref):
    h = _rms(x_ref[...], g_ref[...]).astype(BF16)
    kv = jnp.dot(h, w_ref[...], preferred_element_type=F32)
    o_ref[:, :LANES] = _rope(kv[:, :LANES], cos_ref[...], sin_ref[...])
    o_ref[:, LANES:] = kv[:, LANES:]


def kv_proj(x, g, w, cosp, sinp, *, tm):
    t, d = x.shape
    n = w.shape[1]
    row = lambda i: (i, 0)
    const = lambda i: (0, 0)
    return pl.pallas_call(
        _kv_proj_kernel,
        grid=(t // tm,),
        in_specs=[pl.BlockSpec((tm, d), row), pl.BlockSpec((1, d), const), pl.BlockSpec((d, n), const),
                  pl.BlockSpec((tm, LANES), row), pl.BlockSpec((tm, LANES), row)],
        out_specs=pl.BlockSpec((tm, n), row),
        out_shape=jax.ShapeDtypeStruct((t, n), F32),
        compiler_params=_cparams(("parallel",)),
        name="kv_proj",
    )(x, g.reshape(1, d), w, cosp, sinp)


def _level_masks():
    ri = lax.broadcasted_iota(jnp.int32, (CHUNK, CHUNK), 0)
    ci = lax.broadcasted_iota(jnp.int32, (CHUNK, CHUNK), 1)
    masks = []
    for l in range(1, 7):
        same = (ri >> l) == (ci >> l)
        diff_half = (ri >> (l - 1)) != (ci >> (l - 1))
        masks.append((same & diff_half & (ri > ci)).astype(F32))
    return ri, ci, masks


def _bmm(a, b):
    return jnp.einsum("gmk,gkn->gmn", a, b, preferred_element_type=F32)


def _unit_lower_inverse(lmat, ri, ci, masks):
    eye = (ri == ci).astype(F32)
    p = eye - lmat * masks[0]
    for l in range(1, 6):
        e = (lmat * masks[l]).astype(BF16)
        pb = p.astype(BF16)
        pe = _bmm(pb, e)
        p = p - _bmm(pe.astype(BF16), pb)
    return p


def _gdn_kernel(qkv_ref, z_ref, ba_ref, cw_ref, alog_ref, dtb_ref, ng_ref, o_ref,
                state_ref, carry_ref, *, rows):
    t = pl.program_id(1)
    nc = rows // CHUNK
    nh = GDN_HEADS
    hw = nh * GDN_D

    @pl.when(t == 0)
    def _():
        state_ref[...] = jnp.zeros_like(state_ref)
        carry_ref[...] = jnp.zeros_like(carry_ref)

    ri, ci, masks = _level_masks()
    tril = ri >= ci
    strict = (ri > ci).astype(F32)

    ba = ba_ref[...]
    beta_all = _sigmoid(ba)
    sp_in = ba + dtb_ref[...]
    softplus = jnp.maximum(sp_in, 0.0) + jnp.log(1.0 + jnp.exp(-jnp.abs(sp_in)))
    g_all = -jnp.exp(alog_ref[...]) * softplus
    ti = lax.broadcasted_iota(jnp.int32, (rows, rows), 0)
    tj = lax.broadcasted_iota(jnp.int32, (rows, rows), 1)
    tri = (((ti // CHUNK) == (tj // CHUNK)) & (tj <= ti)).astype(BF16)
    g_hi = g_all.astype(BF16)
    rem = g_all - g_hi.astype(F32)
    g_mid = rem.astype(BF16)
    g_lo = (rem - g_mid.astype(F32)).astype(BF16)
    gsum = jnp.dot(tri, jnp.concatenate([g_hi, g_mid, g_lo], axis=1), preferred_element_type=F32)
    gc_all = gsum[:, :LANES] + gsum[:, LANES:2 * LANES] + gsum[:, 2 * LANES:]
    gc_t = jnp.transpose(gc_all)

    row8 = lax.broadcasted_iota(jnp.int32, (8, GDN_D), 0)

    def conv_silu(col):
        x = qkv_ref[:, col:col + GDN_D]
        c8 = carry_ref[:, col:col + GDN_D]
        acc = x * cw_ref[CONV_K - 1:CONV_K, col:col + GDN_D]
        for k in range(1, CONV_K):
            xr = pltpu.roll(x, k, axis=0)
            cr = pltpu.roll(c8, k, axis=0)
            head = jnp.where(row8 < k, cr, xr[:8])
            sh = jnp.concatenate([head, xr[8:]], axis=0)
            acc = acc + sh * cw_ref[CONV_K - 1 - k:CONV_K - k, col:col + GDN_D]
        return _silu_of_half(acc)

    parts = {name: [] for name in ("q", "k", "kb", "rhs", "qg", "kg", "gcb", "gcrow", "glast")}
    for h in range(nh):
        q = conv_silu(h * GDN_D)
        k = conv_silu(hw + h * GDN_D)
        v = conv_silu(2 * hw + h * GDN_D)
        q = q * lax.rsqrt(jnp.sum(q * q, axis=-1, keepdims=True) + EPS) * (GDN_D ** -0.5)
        k = k * lax.rsqrt(jnp.sum(k * k, axis=-1, keepdims=True) + EPS)
        beta = jnp.broadcast_to(beta_all[:, h:h + 1], (rows, GDN_D))
        gc = jnp.broadcast_to(gc_all[:, nh + h:nh + h + 1], (rows, GDN_D))
        gc3 = gc.reshape(nc, CHUNK, GDN_D)
        glast = jnp.broadcast_to(gc3[:, CHUNK - 1:CHUNK, :], (nc, CHUNK, GDN_D))
        eg = jnp.exp(gc)
        kb = k * beta
        three = lambda a: a.reshape(nc, CHUNK, a.shape[-1])
        parts["q"].append(three(q.astype(BF16)))
        parts["k"].append(three(k.astype(BF16)))
        parts["kb"].append(three(kb.astype(BF16)))
        parts["rhs"].append(three(jnp.concatenate([v * beta, kb * eg], axis=1).astype(BF16)))
        parts["qg"].append(three((q * eg).astype(BF16)))
        parts["kg"].append(three(k) * jnp.exp(glast - gc3))
        parts["gcb"].append(gc3[:, :, :CHUNK])
        parts["glast"].append(glast[:, :1, :])
        for c in range(nc):
            parts["gcrow"].append(
                gc_t[nh + h:nh + h + 1, c * CHUNK:(c + 1) * CHUNK].reshape(1, 1, CHUNK))
    st = {name: jnp.concatenate(v, axis=0) for name, v in parts.items()}

    decay = jnp.exp(jnp.where(tril, st["gcb"] - st["gcrow"], -jnp.inf))
    kq = jnp.einsum("gmd,gnd->gmn", jnp.concatenate([st["kb"], st["q"]], axis=1), st["k"],
                    preferred_element_type=F32)
    lmat = kq[:, :CHUNK] * (decay * strict)
    intra = (kq[:, CHUNK:] * decay).astype(BF16)
    tinv = _unit_lower_inverse(lmat, ri, ci, masks)
    sol = _bmm(tinv.astype(BF16), st["rhs"])

    four = lambda a: a.reshape((nh, nc) + a.shape[1:])
    u4 = four(sol[:, :, :GDN_D])
    w4 = four(sol[:, :, GDN_D:].astype(BF16))
    qg4, kg4, intra4 = four(st["qg"]), four(st["kg"]), four(intra)
    egl4 = four(jnp.exp(st["glast"]))
    zg = _silu_of_half(z_ref[...])
    state = state_ref[...]
    for c in range(nc):
        wq = jnp.concatenate([w4[:, c], qg4[:, c]], axis=1)
        ws_qs = _bmm(wq, state.astype(BF16))
        v_new = (u4[:, c] - ws_qs[:, :CHUNK]).astype(BF16)
        o_c = ws_qs[:, CHUNK:] + _bmm(intra4[:, c], v_new)
        kgt = jnp.stack([jnp.transpose(kg4[h, c]) for h in range(nh)], axis=0).astype(BF16)
        state = state * egl4[:, c] + _bmm(kgt, v_new)
        sl = slice(c * CHUNK, (c + 1) * CHUNK)
        for h in range(nh):
            cs = slice(h * GDN_D, (h + 1) * GDN_D)
            o_ref[sl, cs] = _rms(o_c[h], ng_ref[...]) * zg[sl, cs]
    state_ref[...] = state
    carry_ref[...] = qkv_ref[rows - 8:rows, :]


def gdn(proj, conv_w, a_log_p, dt_bias_p, norm_g, *, batch, seq, rows):
    t = batch * seq
    nt = seq // rows
    hw = GDN_HEADS * GDN_D
    kern = functools.partial(_gdn_kernel, rows=rows)
    const = lambda b, i: (0, 0)
    return pl.pallas_call(
        kern,
        grid=(batch, nt),
        in_specs=[pl.BlockSpec((rows, 3 * hw), lambda b, i: (b * nt + i, 0)),
                  pl.BlockSpec((rows, hw), lambda b, i: (b * nt + i, 3)),
                  pl.BlockSpec((rows, LANES), lambda b, i: (b * nt + i, (4 * hw + 256) // LANES)),
                  pl.BlockSpec((CONV_K, 3 * hw), const),
                  pl.BlockSpec((1, LANES), const),
                  pl.BlockSpec((1, LANES), const),
                  pl.BlockSpec((1, GDN_D), const)],
        out_specs=pl.BlockSpec((rows, hw), lambda b, i: (b * nt + i, 0)),
        out_shape=jax.ShapeDtypeStruct((t, hw), F32),
        scratch_shapes=[pltpu.VMEM((GDN_HEADS, GDN_D, GDN_D), F32),
                        pltpu.VMEM((8, 3 * hw), F32)],
        compiler_params=_cparams(("arbitrary", "arbitrary")),
        name="gdn",
    )(proj, proj, proj, conv_w, a_log_p, dt_bias_p, norm_g.reshape(1, GDN_D))


def _swa_kernel(sink_ref, q_ref, cos_ref, sin_ref, kvc_ref, kvp_ref, o_ref, *, qrows):
    t = pl.program_id(1)
    nsub = qrows // SWA_BLOCK
    lane = lax.broadcasted_iota(jnp.int32, (1, LANES), 1)
    lo = lane < SWA_DH
    qi = lax.broadcasted_iota(jnp.int32, (SWA_BLOCK, 2 * SWA_BLOCK), 0)
    ki = lax.broadcasted_iota(jnp.int32, (SWA_BLOCK, 2 * SWA_BLOCK), 1)
    band = (ki > qi) & (ki <= qi + SWA_BLOCK)
    band_first = band & (ki >= jnp.where(t > 0, 0, SWA_BLOCK))
    dn = (((1,), (1,)), ((), ()))
    pairs_per_kv = SWA_HEADS // SWA_KV_HEADS // 2
    ones = jnp.ones((2 * SWA_BLOCK, LANES), BF16)

    kv_all = jnp.concatenate([kvp_ref[...], kvc_ref[...]], axis=0)
    kblk = kv_all[:, :LANES]
    vblk = kv_all[:, LANES:]
    kroll = pltpu.roll(kblk, SWA_DH, axis=1)
    vroll = pltpu.roll(vblk, SWA_DH, axis=1)
    kexp = [(jnp.where(lo, kblk, 0.0).astype(BF16), jnp.where(lo, 0.0, kroll).astype(BF16)),
            (jnp.where(lo, kroll, 0.0).astype(BF16), jnp.where(lo, 0.0, kblk).astype(BF16))]
    vexp = [(jnp.where(lo, vblk, 0.0).astype(BF16), jnp.where(lo, 0.0, vroll).astype(BF16)),
            (jnp.where(lo, vroll, 0.0).astype(BF16), jnp.where(lo, 0.0, vblk).astype(BF16))]

    for j in range(nsub):
        ws = slice(j * SWA_BLOCK, (j + 2) * SWA_BLOCK)
        rs = slice(j * SWA_BLOCK, (j + 1) * SWA_BLOCK)
        mask = band_first if j == 0 else band
        cosp, sinp = cos_ref[rs, :], sin_ref[rs, :]
        groups = [(g, half) for g in range(SWA_KV_HEADS) for half in range(2)]
        pairs = lambda g: [g * pairs_per_kv + i for i in range(pairs_per_kv)]
        qs = [jnp.concatenate(
            [(_rope(q_ref[rs, p * LANES:(p + 1) * LANES], cosp, sinp) * (SWA_DH ** -0.5)).astype(BF16)
             for p in pairs(g)], axis=0) for g in range(SWA_KV_HEADS)]
        scs = [lax.dot_general(qs[g], kexp[g][half][ws], dn, preferred_element_type=F32)
               for g, half in groups]
        es, sink_terms = [], []
        for (g, half), sc in zip(groups, scs):
            e_g, s_g = [], []
            for i, p in enumerate(pairs(g)):
                sink = sink_ref[2 * p + half]
                s_i = jnp.where(mask, sc[i * SWA_BLOCK:(i + 1) * SWA_BLOCK], -jnp.inf)
                m = jnp.maximum(jnp.max(s_i, axis=-1, keepdims=True), sink)
                e_g.append(jnp.exp(s_i - m).astype(BF16))
                s_g.append(jnp.broadcast_to(jnp.exp(sink - m), (SWA_BLOCK, LANES)))
            es.append(jnp.concatenate(e_g, axis=0))
            sink_terms.append(jnp.concatenate(s_g, axis=0))
        pvs = [jnp.dot(es[n], vexp[g][half][ws], preferred_element_type=F32)
               / (jnp.dot(es[n], ones, preferred_element_type=F32) + sink_terms[n])
               for n, (g, half) in enumerate(groups)]
        for g in range(SWA_KV_HEADS):
            acc = pvs[2 * g] + pvs[2 * g + 1]
            for i, p in enumerate(pairs(g)):
                o_ref[rs, p * LANES:(p + 1) * LANES] = acc[i * SWA_BLOCK:(i + 1) * SWA_BLOCK]


def swa(proj, kv, sinks, cosp, sinp, *, batch, seq, qrows):
    t = batch * seq
    nt = seq // qrows
    qw = SWA_HEADS * SWA_DH
    nsub = qrows // SWA_BLOCK
    nblk = seq // SWA_BLOCK
    row = lambda b, i: (b * nt + i, 0)
    kern = functools.partial(_swa_kernel, qrows=qrows)
    return pl.pallas_call(
        kern,
        grid=(batch, nt),
        in_specs=[pl.BlockSpec(memory_space=pltpu.SMEM),
                  pl.BlockSpec((qrows, qw), row),
                  pl.BlockSpec((qrows, LANES), row), pl.BlockSpec((qrows, LANES), row),
                  pl.BlockSpec((qrows, 2 * LANES), row),
                  pl.BlockSpec((SWA_BLOCK, 2 * LANES),
                               lambda b, i: (b * nblk + jnp.maximum(i * nsub - 1, 0), 0))],
        out_specs=pl.BlockSpec((qrows, qw), row),
        out_shape=jax.ShapeDtypeStruct((t, qw), F32),
        compiler_params=_cparams(("parallel", "arbitrary")),
        name="swa",
    )(sinks, proj, cosp, sinp, kv, kv)


def _mix_ffn_kernel(x_ref, mix_ref, mq_ref, mkv_ref, wo_ref, g_ref, wg_ref, wu_ref, wd_ref, gf_ref,
                    o_ref, acc_ref, *, ff_tile, final):
    lane = lax.broadcasted_iota(jnp.int32, (1, LANES), 1)
    lo = lane < MEM_DH
    mw = MEM_HEADS * MEM_DH
    dn = (((1,), (1,)), ((), ()))
    outs = [mix_ref[...].astype(BF16)]
    for p in range(MEM_HEADS // 2):
        qp = (mq_ref[:, p * LANES:(p + 1) * LANES] * (MEM_DH ** -0.5)).astype(BF16)
        kp = mkv_ref[:, p * LANES:(p + 1) * LANES]
        vp = mkv_ref[:, mw + p * LANES:mw + (p + 1) * LANES]
        acc = None
        for half in range(2):
            sel = lo if half == 0 else jnp.logical_not(lo)
            kh = jnp.where(sel, kp, 0.0).astype(BF16)
            vh = jnp.where(sel, vp, 0.0).astype(BF16)
            sc = lax.dot_general(qp, kh, dn, preferred_element_type=F32)
            m = jnp.max(sc, axis=-1, keepdims=True)
            e = jnp.exp(sc - m)
            scale = jnp.broadcast_to(1.0 / jnp.sum(e, axis=-1, keepdims=True), (e.shape[0], LANES))
            pv = jnp.dot(e.astype(BF16), vh, preferred_element_type=F32) * scale
            acc = pv if acc is None else acc + pv
        outs.append(acc.astype(BF16))
    cat = jnp.concatenate(outs, axis=1)
    x = x_ref[...] + jnp.dot(cat, wo_ref[...], preferred_element_type=F32)
    h = _rms(x, g_ref[...]).astype(BF16)
    d_ff = wg_ref.shape[1]
    acc_ref[...] = x
    for j in range(d_ff // ff_tile):
        cs = slice(j * ff_tile, (j + 1) * ff_tile)
        gate = jnp.dot(h, wg_ref[:, cs], preferred_element_type=F32)
        up = jnp.dot(h, wu_ref[:, cs], preferred_element_type=F32)
        act = (_silu(gate) * up).astype(BF16)
        acc_ref[...] += jnp.dot(act, wd_ref[cs, :], preferred_element_type=F32)
    y = acc_ref[...]
    o_ref[...] = _rms(y, gf_ref[...]) if final else y


def mix_ffn(x, mix, proj, mq_block, mkv, w_out, g, wgu, wd, g_final, *, layer, batch, seq, tm, ff_tile, final):
    t, d = x.shape
    nt = seq // tm
    mixw = mix.shape[1]
    mw = MEM_HEADS * MEM_DH
    mlen = mkv.shape[0] // batch
    d_ff = wd.shape[1]
    row = lambda b, i: (b * nt + i, 0)
    const = lambda b, i: (0, 0)
    single = pl.Buffered(1)
    kern = functools.partial(_mix_ffn_kernel, ff_tile=ff_tile, final=final)
    return pl.pallas_call(
        kern,
        grid=(batch, nt),
        in_specs=[pl.BlockSpec((tm, d), row),
                  pl.BlockSpec((tm, mixw), row),
                  pl.BlockSpec((tm, mw), lambda b, i: (b * nt + i, mq_block)),
                  pl.BlockSpec((mlen, 2 * mw), lambda b, i: (b, 0)),
                  pl.BlockSpec((None, mixw + mw, d), lambda b, i: (layer, 0, 0), pipeline_mode=single),
                  pl.BlockSpec((1, d), const),
                  pl.BlockSpec((None, d, d_ff), lambda b, i: (layer, 0, 0), pipeline_mode=single),
                  pl.BlockSpec((None, d, d_ff), lambda b, i: (layer, 0, 1), pipeline_mode=single),
                  pl.BlockSpec((None, d_ff, d), lambda b, i: (layer, 0, 0), pipeline_mode=single),
                  pl.BlockSpec((1, d), const)],
        out_specs=pl.BlockSpec((tm, d), row),
        out_shape=jax.ShapeDtypeStruct((t, d), F32),
        scratch_shapes=[pltpu.VMEM((tm, d), F32)],
        compiler_params=_cparams(("parallel", "parallel")),
        name="mix_ffn",
    )(x, mix, proj, mkv, w_out, g.reshape(1, d), wgu, wgu, wd, g_final.reshape(1, d))


def _rope_table_kernel(pos_ref, inv_ref, cos_ref, sin_ref):
    half = ROT_DIM // 2
    lane = lax.broadcasted_iota(jnp.int32, (1, LANES), 1) & (SWA_DH - 1)
    for r in range(pos_ref.shape[0]):
        prow = pos_ref[r:r + 1, :].astype(F32)
        pcol = jnp.transpose(jnp.broadcast_to(prow, (LANES, LANES)))
        ang = pcol * inv_ref[...]
        sin = jnp.sin(ang)
        rs = slice(r * LANES, (r + 1) * LANES)
        cos_ref[rs, :] = jnp.where(lane < ROT_DIM, jnp.cos(ang), 1.0)
        sin_ref[rs, :] = jnp.where(lane < half, -sin, jnp.where(lane < ROT_DIM, sin, 0.0))


def rope_tables(positions):
    half = ROT_DIM // 2
    t = positions.size
    rows = 8
    inv = ROPE_THETA ** (-jnp.arange(0, ROT_DIM, 2, dtype=F32) / ROT_DIM)
    inv_lane = inv[(jnp.arange(LANES) % SWA_DH) % half].reshape(1, LANES)
    out = jax.ShapeDtypeStruct((t, LANES), F32)
    return pl.pallas_call(
        _rope_table_kernel,
        grid=(t // (rows * LANES),),
        in_specs=[pl.BlockSpec((rows, LANES), lambda i: (i, 0)), pl.BlockSpec((1, LANES), lambda i: (0, 0))],
        out_specs=[pl.BlockSpec((rows * LANES, LANES), lambda i: (i, 0))] * 2,
        out_shape=[out, out],
        compiler_params=_cparams(("parallel",)),
        name="rope_tables",
    )(positions.reshape(t // LANES, LANES), inv_lane)


def _pad_lanes(v, offset):
    out = jnp.zeros((1, LANES), F32)
    return lax.dynamic_update_slice(out, v.reshape(1, -1).astype(F32), (0, offset))


def kernel(x, mem, positions, ln_mix, ln_ffn, ln_mem, w_mem_kv, w_out, w_gate_up, w_down,
           gdn_w_in, gdn_conv, gdn_A_log, gdn_dt_bias, gdn_norm,
           swa_w_q, swa_sinks, ln_kv, w_kv, ln_final):
    batch, seq, d = x.shape
    depth = ln_mix.shape[0]
    n_a = gdn_w_in.shape[0]
    t = batch * seq
    mlen = mem.shape[1]
    hw = GDN_HEADS * GDN_D
    mw = MEM_HEADS * MEM_DH

    cosp, sinp = rope_tables(positions)
    xs = x.reshape(t, d)
    mem2 = mem.reshape(batch * mlen, d)
    w_mem_kv, w_out, w_gate_up, w_down, swa_w_q = (
        w.astype(BF16) for w in (w_mem_kv, w_out, w_gate_up, w_down, swa_w_q))
    kv = None
    for layer in range(depth):
        mkv = norm_matmul(mem2, ln_mem, w_mem_kv, layer=layer, tm=batch * mlen)
        if layer < n_a:
            a = layer
            o2 = 4 * hw
            w_r = gdn_weight_prep(gdn_w_in, a, o2=o2, n_gate=2 * GDN_HEADS, tr=256)
            proj = norm_matmul(xs, ln_mix[layer], w_r, tm=512)
            mix = gdn(proj, 0.5 * gdn_conv[a], _pad_lanes(gdn_A_log[a], GDN_HEADS),
                      _pad_lanes(gdn_dt_bias[a], GDN_HEADS), gdn_norm[a],
                      batch=batch, seq=seq, rows=256)
            mq_block = o2 // mw
        else:
            bl = layer - n_a
            proj = norm_matmul(xs, ln_mix[layer], swa_w_q, layer=bl, tm=512)
            mix = swa(proj, kv, swa_sinks[bl], cosp, sinp, batch=batch, seq=seq, qrows=512)
            mq_block = (SWA_HEADS * SWA_DH) // mw
        xs = mix_ffn(xs, mix, proj, mq_block, mkv, w_out, ln_ffn[layer], w_gate_up, w_down, ln_final,
                     layer=layer, batch=batch, seq=seq, tm=512, ff_tile=256, final=(layer == depth - 1))
        if layer == n_a - 1:
            kv = kv_proj(xs, ln_kv, w_kv.astype(BF16), cosp, sinp, tm=512)
    return xs.reshape(batch, seq, d)
```

```python
import functools

import jax
import jax.numpy as jnp
from jax import lax
from jax.experimental import pallas as pl
from jax.experimental.pallas import tpu as pltpu

F32 = jnp.float32
BF16 = jnp.bfloat16

EPS = 1e-6
LANES = 128
CHUNK = 64
CONV_K = 4
GDN_HEADS = 6
GDN_D = 128
SWA_HEADS = 12
SWA_KV_HEADS = 2
SWA_DH = 64
SWA_BLOCK = 128
ROT_DIM = 16
ROPE_THETA = 500000.0
MEM_HEADS = 4
MEM_DH = 64
VMEM_LIMIT = 56 * 1024 * 1024


def _cparams(sem):
    return pltpu.CompilerParams(dimension_semantics=sem, vmem_limit_bytes=VMEM_LIMIT)


def _rms(x, g):
    ms = jnp.mean(x * x, axis=-1, keepdims=True)
    return x * lax.rsqrt(ms + EPS) * g


def _sigmoid(x):
    return 0.5 + 0.5 * jnp.tanh(0.5 * x)


def _silu_of_half(h):
    return h + h * jnp.tanh(h)


def _silu(x):
    return _silu_of_half(0.5 * x)


def _rope(x, cosp, sinp):
    half = ROT_DIM // 2
    lane = lax.broadcasted_iota(jnp.int32, (1, LANES), 1) & (SWA_DH - 1)
    partner = jnp.where(lane < half, pltpu.roll(x, LANES - half, axis=1), pltpu.roll(x, half, axis=1))
    return x * cosp + partner * sinp


def _norm_matmul_kernel(x_ref, g_ref, w_ref, o_ref):
    h = _rms(x_ref[...], g_ref[...]).astype(BF16)
    o_ref[...] = jnp.dot(h, w_ref[...], preferred_element_type=F32)


def norm_matmul(x, g, w, *, tm, layer=None):
    t, d = x.shape
    n = w.shape[-1]
    if layer is None:
        w_spec = pl.BlockSpec((d, n), lambda i: (0, 0))
    else:
        w_spec = pl.BlockSpec((None, d, n), lambda i: (layer, 0, 0))
    return pl.pallas_call(
        _norm_matmul_kernel,
        grid=(t // tm,),
        in_specs=[pl.BlockSpec((tm, d), lambda i: (i, 0)), pl.BlockSpec((1, d), lambda i: (0, 0)), w_spec],
        out_specs=pl.BlockSpec((tm, n), lambda i: (i, 0)),
        out_shape=jax.ShapeDtypeStruct((t, n), F32),
        compiler_params=_cparams(("parallel",)),
        name="norm_matmul",
    )(x, g.reshape(1, d), w)


def _gdn_weight_prep_kernel(w_ref, o_ref, *, o2, n_gate):
    w = w_ref[...]
    n_in = w.shape[1]
    mq = n_in - o2 - n_gate
    zc = 3 * o2 // 4
    o_ref[:, :zc] = w[:, :zc].astype(BF16)
    o_ref[:, zc:o2] = (0.5 * w[:, zc:o2]).astype(BF16)
    o_ref[:, o2:o2 + mq] = w[:, o2 + n_gate:].astype(BF16)
    gates = jnp.concatenate([w[:, o2:o2 + n_gate], jnp.zeros((w.shape[0], LANES - n_gate), F32)], axis=1)
    o_ref[:, o2 + mq:] = gates.astype(BF16)


def gdn_weight_prep(w_in_all, layer, *, o2, n_gate, tr):
    _, d, n_in = w_in_all.shape
    n_out = n_in - n_gate + LANES
    kern = functools.partial(_gdn_weight_prep_kernel, o2=o2, n_gate=n_gate)
    return pl.pallas_call(
        kern,
        grid=(d // tr,),
        in_specs=[pl.BlockSpec((None, tr, n_in), lambda i: (layer, i, 0))],
        out_specs=pl.BlockSpec((tr, n_out), lambda i: (i, 0)),
        out_shape=jax.ShapeDtypeStruct((d, n_out), BF16),
        compiler_params=_cparams(("parallel",)),
        name="gdn_weight_prep",
    )(w_in_all)


def _kv_proj_kernel(x_ref, g_ref, w_ref, cos_ref, sin_ref, o_ref):
    h = _rms(x_ref[...], g_ref[...]).astype(BF16)
    kv = jnp.dot(h, w_ref[...], preferred_element_type=F32)
    o_ref[:, :LANES] = _rope(kv[:, :LANES], cos_ref[...], sin_ref[...])
    o_ref[:, LANES:] = kv[:, LANES:]


def kv_proj(x, g, w, cosp, sinp, *, tm):
    t, d = x.shape
    n = w.shape[1]
    row = lambda i: (i, 0)
    const = lambda i: (0, 0)
    return pl.pallas_call(
        _kv_proj_kernel,
        grid=(t // tm,),
        in_specs=[pl.BlockSpec((tm, d), row), pl.BlockSpec((1, d), const), pl.BlockSpec((d, n), const),
                  pl.BlockSpec((tm, LANES), row), pl.BlockSpec((tm, LANES), row)],
        out_specs=pl.BlockSpec((tm, n), row),
        out_shape=jax.ShapeDtypeStruct((t, n), F32),
        compiler_params=_cparams(("parallel",)),
        name="kv_proj",
    )(x, g.reshape(1, d), w, cosp, sinp)


def _level_masks():
    ri = lax.broadcasted_iota(jnp.int32, (CHUNK, CHUNK), 0)
    ci = lax.broadcasted_iota(jnp.int32, (CHUNK, CHUNK), 1)
    masks = []
    for l in range(1, 7):
        same = (ri >> l) == (ci >> l)
        diff_half = (ri >> (l - 1)) != (ci >> (l - 1))
        masks.append((same & diff_half & (ri > ci)).astype(F32))
    return ri, ci, masks


def _bmm(a, b):
    return jnp.einsum("gmk,gkn->gmn", a, b, preferred_element_type=F32)


def _unit_lower_inverse(lmat, ri, ci, masks):
    eye = (ri == ci).astype(F32)
    p = eye - lmat * masks[0]
    for l in range(1, 6):
        e = (lmat * masks[l]).astype(BF16)
        pb = p.astype(BF16)
        pe = _bmm(pb, e)
        p = p - _bmm(pe.astype(BF16), pb)
    return p


def _gdn_kernel(qkv_ref, z_ref, ba_ref, cw_ref, alog_ref, dtb_ref, ng_ref, o_ref,
                state_ref, carry_ref, *, rows):
    t = pl.program_id(1)
    nc = rows // CHUNK
    nh = GDN_HEADS
    hw = nh * GDN_D

    @pl.when(t == 0)
    def _():
        state_ref[...] = jnp.zeros_like(state_ref)
        carry_ref[...] = jnp.zeros_like(carry_ref)

    ri, ci, masks = _level_masks()
    tril = ri >= ci
    strict = (ri > ci).astype(F32)

    ba = ba_ref[...]
    beta_all = _sigmoid(ba)
    sp_in = ba + dtb_ref[...]
    softplus = jnp.maximum(sp_in, 0.0) + jnp.log(1.0 + jnp.exp(-jnp.abs(sp_in)))
    g_all = -jnp.exp(alog_ref[...]) * softplus
    ti = lax.broadcasted_iota(jnp.int32, (rows, rows), 0)
    tj = lax.broadcasted_iota(jnp.int32, (rows, rows), 1)
    tri = (((ti // CHUNK) == (tj // CHUNK)) & (tj <= ti)).astype(BF16)
    g_hi = g_all.astype(BF16)
    rem = g_all - g_hi.astype(F32)
    g_mid = rem.astype(BF16)
    g_lo = (rem - g_mid.astype(F32)).astype(BF16)
    gsum = jnp.dot(tri, jnp.concatenate([g_hi, g_mid, g_lo], axis=1), preferred_element_type=F32)
    gc_all = gsum[:, :LANES] + gsum[:, LANES:2 * LANES] + gsum[:, 2 * LANES:]
    gc_t = jnp.transpose(gc_all)

    row8 = lax.broadcasted_iota(jnp.int32, (8, GDN_D), 0)

    def conv_silu(col):
        x = qkv_ref[:, col:col + GDN_D]
        c8 = carry_ref[:, col:col + GDN_D]
        acc = x * cw_ref[CONV_K - 1:CONV_K, col:col + GDN_D]
        for k in range(1, CONV_K):
            xr = pltpu.roll(x, k, axis=0)
            cr = pltpu.roll(c8, k, axis=0)
            head = jnp.where(row8 < k, cr, xr[:8])
            sh = jnp.concatenate([head, xr[8:]], axis=0)
            acc = acc + sh * cw_ref[CONV_K - 1 - k:CONV_K - k, col:col + GDN_D]
        return _silu_of_half(acc)

    parts = {name: [] for name in ("q", "k", "kb", "rhs", "qg", "kg", "gcb", "gcrow", "glast")}
    for h in range(nh):
        q = conv_silu(h * GDN_D)
        k = conv_silu(hw + h * GDN_D)
        v = conv_silu(2 * hw + h * GDN_D)
        q = q * lax.rsqrt(jnp.sum(q * q, axis=-1, keepdims=True) + EPS) * (GDN_D ** -0.5)
        k = k * lax.rsqrt(jnp.sum(k * k, axis=-1, keepdims=True) + EPS)
        beta = jnp.broadcast_to(beta_all[:, h:h + 1], (rows, GDN_D))
        gc = jnp.broadcast_to(gc_all[:, nh + h:nh + h + 1], (rows, GDN_D))
        gc3 = gc.reshape(nc, CHUNK, GDN_D)
        glast = jnp.broadcast_to(gc3[:, CHUNK - 1:CHUNK, :], (nc, CHUNK, GDN_D))
        eg = jnp.exp(gc)
        kb = k * beta
        three = lambda a: a.reshape(nc, CHUNK, a.shape[-1])
        parts["q"].append(three(q.astype(BF16)))
        parts["k"].append(three(k.astype(BF16)))
        parts["kb"].append(three(kb.astype(BF16)))
        parts["rhs"].append(three(jnp.concatenate([v * beta, kb * eg], axis=1).astype(BF16)))
        parts["qg"].append(three((q * eg).astype(BF16)))
        parts["kg"].append(three(k) * jnp.exp(glast - gc3))
        parts["gcb"].append(gc3[:, :, :CHUNK])
        parts["glast"].append(glast[:, :1, :])
        for c in range(nc):
            parts["gcrow"].append(
                gc_t[nh + h:nh + h + 1, c * CHUNK:(c + 1) * CHUNK].reshape(1, 1, CHUNK))
    st = {name: jnp.concatenate(v, axis=0) for name, v in parts.items()}

    decay = jnp.exp(jnp.where(tril, st["gcb"] - st["gcrow"], -jnp.inf))
    kq = jnp.einsum("gmd,gnd->gmn", jnp.concatenate([st["kb"], st["q"]], axis=1), st["k"],
                    preferred_element_type=F32)
    lmat = kq[:, :CHUNK] * (decay * strict)
    intra = (kq[:, CHUNK:] * decay).astype(BF16)
    tinv = _unit_lower_inverse(lmat, ri, ci, masks)
    sol = _bmm(tinv.astype(BF16), st["rhs"])

    four = lambda a: a.reshape((nh, nc) + a.shape[1:])
    u4 = four(sol[:, :, :GDN_D])
    w4 = four(sol[:, :, GDN_D:].astype(BF16))
    qg4, kg4, intra4 = four(st["qg"]), four(st["kg"]), four(intra)
    egl4 = four(jnp.exp(st["glast"]))
    zg = _silu_of_half(z_ref[...])
    state = state_ref[...]
    for c in range(nc):
        wq = jnp.concatenate([w4[:, c], qg4[:, c]], axis=1)
        ws_qs = _bmm(wq, state.astype(BF16))
        v_new = (u4[:, c] - ws_qs[:, :CHUNK]).astype(BF16)
        o_c = ws_qs[:, CHUNK:] + _bmm(intra4[:, c], v_new)
        kgt = jnp.stack([jnp.transpose(kg4[h, c]) for h in range(nh)], axis=0).astype(BF16)
        state = state * egl4[:, c] + _bmm(kgt, v_new)
        sl = slice(c * CHUNK, (c + 1) * CHUNK)
        for h in range(nh):
            cs = slice(h * GDN_D, (h + 1) * GDN_D)
            o_ref[sl, cs] = _rms(o_c[h], ng_ref[...]) * zg[sl, cs]
    state_ref[...] = state
    carry_ref[...] = qkv_ref[rows - 8:rows, :]


def gdn(proj, conv_w, a_log_p, dt_bias_p, norm_g, *, batch, seq, rows):
    t = batch * seq
    nt = seq // rows
    hw = GDN_HEADS * GDN_D
    kern = functools.partial(_gdn_kernel, rows=rows)
    const = lambda b, i: (0, 0)
    return pl.pallas_call(
        kern,
        grid=(batch, nt),
        in_specs=[pl.BlockSpec((rows, 3 * hw), lambda b, i: (b * nt + i, 0)),
                  pl.BlockSpec((rows, hw), lambda b, i: (b * nt + i, 3)),
                  pl.BlockSpec((rows, LANES), lambda b, i: (b * nt + i, (4 * hw + 256) // LANES)),
                  pl.BlockSpec((CONV_K, 3 * hw), const),
                  pl.BlockSpec((1, LANES), const),
                  pl.BlockSpec((1, LANES), const),
                  pl.BlockSpec((1, GDN_D), const)],
        out_specs=pl.BlockSpec((rows, hw), lambda b, i: (b * nt + i, 0)),
        out_shape=jax.ShapeDtypeStruct((t, hw), F32),
        scratch_shapes=[pltpu.VMEM((GDN_HEADS, GDN_D, GDN_D), F32),
                        pltpu.VMEM((8, 3 * hw), F32)],
        compiler_params=_cparams(("arbitrary", "arbitrary")),
        name="gdn",
    )(proj, proj, proj, conv_w, a_log_p, dt_bias_p, norm_g.reshape(1, GDN_D))


def _swa_kernel(sink_ref, x_ref, g_ref, wq_ref, cos_ref, sin_ref, kvc_ref, kvp_ref, o_ref, mq_ref, q_ref,
                *, qrows):
    t = pl.program_id(1)
    qw = SWA_HEADS * SWA_DH
    hx = _rms(x_ref[...], g_ref[...]).astype(BF16)
    q_ref[...] = jnp.dot(hx, wq_ref[...], preferred_element_type=F32)
    mq_ref[...] = q_ref[:, qw:]
    nsub = qrows // SWA_BLOCK
    lane = lax.broadcasted_iota(jnp.int32, (1, LANES), 1)
    lo = lane < SWA_DH
    qi = lax.broadcasted_iota(jnp.int32, (SWA_BLOCK, 2 * SWA_BLOCK), 0)
    ki = lax.broadcasted_iota(jnp.int32, (SWA_BLOCK, 2 * SWA_BLOCK), 1)
    band = (ki > qi) & (ki <= qi + SWA_BLOCK)
    band_first = band & (ki >= jnp.where(t > 0, 0, SWA_BLOCK))
    dn = (((1,), (1,)), ((), ()))
    pairs_per_kv = SWA_HEADS // SWA_KV_HEADS // 2
    ones = jnp.ones((2 * SWA_BLOCK, LANES), BF16)

    kv_all = jnp.concatenate([kvp_ref[...], kvc_ref[...]], axis=0)
    kblk = kv_all[:, :LANES]
    vblk = kv_all[:, LANES:]
    kroll = pltpu.roll(kblk, SWA_DH, axis=1)
    vroll = pltpu.roll(vblk, SWA_DH, axis=1)
    kexp = [(jnp.where(lo, kblk, 0.0).astype(BF16), jnp.where(lo, 0.0, kroll).astype(BF16)),
            (jnp.where(lo, kroll, 0.0).astype(BF16), jnp.where(lo, 0.0, kblk).astype(BF16))]
    vexp = [(jnp.where(lo, vblk, 0.0).astype(BF16), jnp.where(lo, 0.0, vroll).astype(BF16)),
            (jnp.where(lo, vroll, 0.0).astype(BF16), jnp.where(lo, 0.0, vblk).astype(BF16))]

    for j in range(nsub):
        ws = slice(j * SWA_BLOCK, (j + 2) * SWA_BLOCK)
        rs = slice(j * SWA_BLOCK, (j + 1) * SWA_BLOCK)
        mask = band_first if j == 0 else band
        cosp, sinp = cos_ref[rs, :], sin_ref[rs, :]
        groups = [(g, half) for g in range(SWA_KV_HEADS) for half in range(2)]
        pairs = lambda g: [g * pairs_per_kv + i for i in range(pairs_per_kv)]
        qs = [jnp.concatenate(
            [(_rope(q_ref[rs, p * LANES:(p + 1) * LANES], cosp, sinp) * (SWA_DH ** -0.5)).astype(BF16)
             for p in pairs(g)], axis=0) for g in range(SWA_KV_HEADS)]
        scs = [lax.dot_general(qs[g], kexp[g][half][ws], dn, preferred_element_type=F32)
               for g, half in groups]
        es, sink_terms = [], []
        for (g, half), sc in zip(groups, scs):
            e_g, s_g = [], []
            for i, p in enumerate(pairs(g)):
                sink = sink_ref[2 * p + half]
                s_i = jnp.where(mask, sc[i * SWA_BLOCK:(i + 1) * SWA_BLOCK], -jnp.inf)
                m = jnp.maximum(jnp.max(s_i, axis=-1, keepdims=True), sink)
                e_g.append(jnp.exp(s_i - m).astype(BF16))
                s_g.append(jnp.broadcast_to(jnp.exp(sink - m), (SWA_BLOCK, LANES)))
            es.append(jnp.concatenate(e_g, axis=0))
            sink_terms.append(jnp.concatenate(s_g, axis=0))
        pvs = []
        for n, (g, half) in enumerate(groups):
            r = jnp.dot(es[n], jnp.concatenate([vexp[g][half][ws], ones], axis=1),
                        preferred_element_type=F32)
            pvs.append(r[:, :LANES] / (r[:, LANES:] + sink_terms[n]))
        for g in range(SWA_KV_HEADS):
            acc = pvs[2 * g] + pvs[2 * g + 1]
            for i, p in enumerate(pairs(g)):
                o_ref[rs, p * LANES:(p + 1) * LANES] = acc[i * SWA_BLOCK:(i + 1) * SWA_BLOCK]


def swa(x, g, wq, layer, kv, sinks, cosp, sinp, *, batch, seq, qrows):
    t, d = x.shape
    nt = seq // qrows
    qw = SWA_HEADS * SWA_DH
    n = wq.shape[-1]
    nsub = qrows // SWA_BLOCK
    nblk = seq // SWA_BLOCK
    row = lambda b, i: (b * nt + i, 0)
    const = lambda b, i: (0, 0)
    kern = functools.partial(_swa_kernel, qrows=qrows)
    return pl.pallas_call(
        kern,
        grid=(batch, nt),
        in_specs=[pl.BlockSpec(memory_space=pltpu.SMEM),
                  pl.BlockSpec((qrows, d), row),
                  pl.BlockSpec((1, d), const),
                  pl.BlockSpec((None, d, n), lambda b, i: (layer, 0, 0)),
                  pl.BlockSpec((qrows, LANES), row), pl.BlockSpec((qrows, LANES), row),
                  pl.BlockSpec((qrows, 2 * LANES), row),
                  pl.BlockSpec((SWA_BLOCK, 2 * LANES),
                               lambda b, i: (b * nblk + jnp.maximum(i * nsub - 1, 0), 0))],
        out_specs=[pl.BlockSpec((qrows, qw), row), pl.BlockSpec((qrows, n - qw), row)],
        out_shape=[jax.ShapeDtypeStruct((t, qw), F32), jax.ShapeDtypeStruct((t, n - qw), F32)],
        scratch_shapes=[pltpu.VMEM((qrows, n), F32)],
        compiler_params=_cparams(("parallel", "arbitrary")),
        name="swa",
    )(sinks, x, g.reshape(1, d), wq, cosp, sinp, kv, kv)


def _mix_ffn_kernel(x_ref, mix_ref, mq_ref, mkv_ref, wo_ref, g_ref, wg_ref, wu_ref, wd_ref, gf_ref,
                    o_ref, acc_ref, *, ff_tile, final):
    lane = lax.broadcasted_iota(jnp.int32, (1, LANES), 1)
    lo = lane < MEM_DH
    mw = MEM_HEADS * MEM_DH
    dn = (((1,), (1,)), ((), ()))
    outs = [mix_ref[...].astype(BF16)]
    for p in range(MEM_HEADS // 2):
        qp = (mq_ref[:, p * LANES:(p + 1) * LANES] * (MEM_DH ** -0.5)).astype(BF16)
        kp = mkv_ref[:, p * LANES:(p + 1) * LANES]
        vp = mkv_ref[:, mw + p * LANES:mw + (p + 1) * LANES]
        acc = None
        for half in range(2):
            sel = lo if half == 0 else jnp.logical_not(lo)
            kh = jnp.where(sel, kp, 0.0).astype(BF16)
            vh = jnp.where(sel, vp, 0.0).astype(BF16)
            sc = lax.dot_general(qp, kh, dn, preferred_element_type=F32)
            m = jnp.max(sc, axis=-1, keepdims=True)
            e = jnp.exp(sc - m)
            scale = jnp.broadcast_to(1.0 / jnp.sum(e, axis=-1, keepdims=True), (e.shape[0], LANES))
            pv = jnp.dot(e.astype(BF16), vh, preferred_element_type=F32) * scale
            acc = pv if acc is None else acc + pv
        outs.append(acc.astype(BF16))
    cat = jnp.concatenate(outs, axis=1)
    x = x_ref[...] + jnp.dot(cat, wo_ref[...], preferred_element_type=F32)
    h = _rms(x, g_ref[...]).astype(BF16)
    d_ff = wg_ref.shape[1]
    acc_ref[...] = x
    for j in range(d_ff // ff_tile):
        cs = slice(j * ff_tile, (j + 1) * ff_tile)
        gate = jnp.dot(h, wg_ref[:, cs], preferred_element_type=F32)
        up = jnp.dot(h, wu_ref[:, cs], preferred_element_type=F32)
        act = (_silu(gate) * up).astype(BF16)
        acc_ref[...] += jnp.dot(act, wd_ref[cs, :], preferred_element_type=F32)
    y = acc_ref[...]
    o_ref[...] = _rms(y, gf_ref[...]) if final else y


def mix_ffn(x, mix, proj, mq_block, mkv, w_out, g, wgu, wd, g_final, *, layer, batch, seq, tm, ff_tile, final):
    t, d = x.shape
    nt = seq // tm
    mixw = mix.shape[1]
    mw = MEM_HEADS * MEM_DH
    mlen = mkv.shape[0] // batch
    d_ff = wd.shape[1]
    row = lambda b, i: (b * nt + i, 0)
    const = lambda b, i: (0, 0)
    single = pl.Buffered(1)
    kern = functools.partial(_mix_ffn_kernel, ff_tile=ff_tile, final=final)
    return pl.pallas_call(
        kern,
        grid=(batch, nt),
        in_specs=[pl.BlockSpec((tm, d), row),
                  pl.BlockSpec((tm, mixw), row),
                  pl.BlockSpec((tm, mw), lambda b, i: (b * nt + i, mq_block)),
                  pl.BlockSpec((mlen, 2 * mw), lambda b, i: (b, 0)),
                  pl.BlockSpec((None, mixw + mw, d), lambda b, i: (layer, 0, 0), pipeline_mode=single),
                  pl.BlockSpec((1, d), const),
                  pl.BlockSpec((None, d, d_ff), lambda b, i: (layer, 0, 0), pipeline_mode=single),
                  pl.BlockSpec((None, d, d_ff), lambda b, i: (layer, 0, 1), pipeline_mode=single),
                  pl.BlockSpec((None, d_ff, d), lambda b, i: (layer, 0, 0), pipeline_mode=single),
                  pl.BlockSpec((1, d), const)],
        out_specs=pl.BlockSpec((tm, d), row),
        out_shape=jax.ShapeDtypeStruct((t, d), F32),
        scratch_shapes=[pltpu.VMEM((tm, d), F32)],
        compiler_params=_cparams(("parallel", "parallel")),
        name="mix_ffn",
    )(x, mix, proj, mkv, w_out, g.reshape(1, d), wgu, wgu, wd, g_final.reshape(1, d))


def _rope_table_kernel(pos_ref, inv_ref, cos_ref, sin_ref):
    half = ROT_DIM // 2
    lane = lax.broadcasted_iota(jnp.int32, (1, LANES), 1) & (SWA_DH - 1)
    for r in range(pos_ref.shape[0]):
        prow = pos_ref[r:r + 1, :].astype(F32)
        pcol = jnp.transpose(jnp.broadcast_to(prow, (LANES, LANES)))
        ang = pcol * inv_ref[...]
        sin = jnp.sin(ang)
        rs = slice(r * LANES, (r + 1) * LANES)
        cos_ref[rs, :] = jnp.where(lane < ROT_DIM, jnp.cos(ang), 1.0)
        sin_ref[rs, :] = jnp.where(lane < half, -sin, jnp.where(lane < ROT_DIM, sin, 0.0))


def rope_tables(positions):
    half = ROT_DIM // 2
    t = positions.size
    rows = 8
    inv = ROPE_THETA ** (-jnp.arange(0, ROT_DIM, 2, dtype=F32) / ROT_DIM)
    inv_lane = inv[(jnp.arange(LANES) % SWA_DH) % half].reshape(1, LANES)
    out = jax.ShapeDtypeStruct((t, LANES), F32)
    return pl.pallas_call(
        _rope_table_kernel,
        grid=(t // (rows * LANES),),
        in_specs=[pl.BlockSpec((rows, LANES), lambda i: (i, 0)), pl.BlockSpec((1, LANES), lambda i: (0, 0))],
        out_specs=[pl.BlockSpec((rows * LANES, LANES), lambda i: (i, 0))] * 2,
        out_shape=[out, out],
        compiler_params=_cparams(("parallel",)),
        name="rope_tables",
    )(positions.reshape(t // LANES, LANES), inv_lane)


def _pad_lanes(v, offset):
    out = jnp.zeros((1, LANES), F32)
    return lax.dynamic_update_slice(out, v.reshape(1, -1).astype(F32), (0, offset))


def kernel(x, mem, positions, ln_mix, ln_ffn, ln_mem, w_mem_kv, w_out, w_gate_up, w_down,
           gdn_w_in, gdn_conv, gdn_A_log, gdn_dt_bias, gdn_norm,
           swa_w_q, swa_sinks, ln_kv, w_kv, ln_final):
    batch, seq, d = x.shape
    depth = ln_mix.shape[0]
    n_a = gdn_w_in.shape[0]
    t = batch * seq
    mlen = mem.shape[1]
    hw = GDN_HEADS * GDN_D
    mw = MEM_HEADS * MEM_DH

    cosp, sinp = rope_tables(positions)
    xs = x.reshape(t, d)
    mem2 = mem.reshape(batch * mlen, d)
    w_mem_kv, w_out, w_gate_up, w_down, swa_w_q = (
        w.astype(BF16) for w in (w_mem_kv, w_out, w_gate_up, w_down, swa_w_q))
    kv = None
    for layer in range(depth):
        mkv = norm_matmul(mem2, ln_mem, w_mem_kv, layer=layer, tm=batch * mlen)
        if layer < n_a:
            a = layer
            o2 = 4 * hw
            w_r = gdn_weight_prep(gdn_w_in, a, o2=o2, n_gate=2 * GDN_HEADS, tr=256)
            proj = norm_matmul(xs, ln_mix[layer], w_r, tm=512)
            mix = gdn(proj, 0.5 * gdn_conv[a], _pad_lanes(gdn_A_log[a], GDN_HEADS),
                      _pad_lanes(gdn_dt_bias[a], GDN_HEADS), gdn_norm[a],
                      batch=batch, seq=seq, rows=256)
            mq_block = o2 // mw
        else:
            bl = layer - n_a
            mix, proj = swa(xs, ln_mix[layer], swa_w_q, bl, kv, swa_sinks[bl], cosp, sinp,
                            batch=batch, seq=seq, qrows=512)
            mq_block = 0
        xs = mix_ffn(xs, mix, proj, mq_block, mkv, w_out, ln_ffn[layer], w_gate_up, w_down, ln_final,
                     layer=layer, batch=batch, seq=seq, tm=512, ff_tile=256, final=(layer == depth - 1))
        if layer == n_a - 1:
            kv = kv_proj(xs, ln_kv, w_kv.astype(BF16), cosp, sinp, tm=512)
    return xs.reshape(batch, seq, d)
```

```python
import functools

import jax
import jax.numpy as jnp
from jax import lax
from jax.experimental import pallas as pl
from jax.experimental.pallas import tpu as pltpu

F32 = jnp.float32
BF16 = jnp.bfloat16

EPS = 1e-6
LANES = 128
CHUNK = 64
CONV_K = 4
GDN_HEADS = 6
GDN_D = 128
SWA_HEADS = 12
SWA_KV_HEADS = 2
SWA_DH = 64
SWA_BLOCK = 128
ROT_DIM = 16
ROPE_THETA = 500000.0
MEM_HEADS = 4
MEM_DH = 64
VMEM_LIMIT = 56 * 1024 * 1024


def _cparams(sem):
    return pltpu.CompilerParams(dimension_semantics=sem, vmem_limit_bytes=VMEM_LIMIT)


def _rms(x, g):
    ms = jnp.mean(x * x, axis=-1, keepdims=True)
    return x * lax.rsqrt(ms + EPS) * g


def _sigmoid(x):
    return 0.5 + 0.5 * jnp.tanh(0.5 * x)


def _silu_of_half(h):
    return h + h * jnp.tanh(h)


def _silu(x):
    return _silu_of_half(0.5 * x)


def _rope(x, cosp, sinp):
    half = ROT_DIM // 2
    lane = lax.broadcasted_iota(jnp.int32, (1, LANES), 1) & (SWA_DH - 1)
    partner = jnp.where(lane < half, pltpu.roll(x, LANES - half, axis=1), pltpu.roll(x, half, axis=1))
    return x * cosp + partner * sinp


def _norm_matmul_kernel(x_ref, g_ref, w_ref, o_ref):
    h = _rms(x_ref[...], g_ref[...]).astype(BF16)
    o_ref[...] = jnp.dot(h, w_ref[...], preferred_element_type=F32)


def norm_matmul(x, g, w, *, tm, layer=None):
    t, d = x.shape
    n = w.shape[-1]
    if layer is None:
        w_spec = pl.BlockSpec((d, n), lambda i: (0, 0))
    else:
        w_spec = pl.BlockSpec((None, d, n), lambda i: (layer, 0, 0))
    return pl.pallas_call(
        _norm_matmul_kernel,
        grid=(t // tm,),
        in_specs=[pl.BlockSpec((tm, d), lambda i: (i, 0)), pl.BlockSpec((1, d), lambda i: (0, 0)), w_spec],
        out_specs=pl.BlockSpec((tm, n), lambda i: (i, 0)),
        out_shape=jax.ShapeDtypeStruct((t, n), F32),
        compiler_params=_cparams(("parallel",)),
        name="norm_matmul",
    )(x, g.reshape(1, d), w)


def _gdn_weight_prep_kernel(w_ref, o_ref, *, o2, n_gate):
    w = w_ref[...]
    n_in = w.shape[1]
    mq = n_in - o2 - n_gate
    zc = 3 * o2 // 4
    o_ref[:, :zc] = w[:, :zc].astype(BF16)
    o_ref[:, zc:o2] = (0.5 * w[:, zc:o2]).astype(BF16)
    o_ref[:, o2:o2 + mq] = w[:, o2 + n_gate:].astype(BF16)
    gates = jnp.concatenate([w[:, o2:o2 + n_gate], jnp.zeros((w.shape[0], LANES - n_gate), F32)], axis=1)
    o_ref[:, o2 + mq:] = gates.astype(BF16)


def gdn_weight_prep(w_in_all, layer, *, o2, n_gate, tr):
    _, d, n_in = w_in_all.shape
    n_out = n_in - n_gate + LANES
    kern = functools.partial(_gdn_weight_prep_kernel, o2=o2, n_gate=n_gate)
    return pl.pallas_call(
        kern,
        grid=(d // tr,),
        in_specs=[pl.BlockSpec((None, tr, n_in), lambda i: (layer, i, 0))],
        out_specs=pl.BlockSpec((tr, n_out), lambda i: (i, 0)),
        out_shape=jax.ShapeDtypeStruct((d, n_out), BF16),
        compiler_params=_cparams(("parallel",)),
        name="gdn_weight_prep",
    )(w_in_all)


def _kv_proj_kernel(x_ref, g_ref, w_ref, cos_ref, sin_ref, o_ref):
    h = _rms(x_ref[...], g_ref[...]).astype(BF16)
    kv = jnp.dot(h, w_ref[...], preferred_element_type=F32)
    o_ref[:, :LANES] = _rope(kv[:, :LANES], cos_ref[...], sin_ref[...])
    o_ref[:, LANES:] = kv[:, LANES:]


def kv_proj(x, g, w, cosp, sinp, *, tm):
    t, d = x.shape
    n = w.shape[1]
    row = lambda i: (i, 0)
    const = lambda i: (0, 0)
    return pl.pallas_call(
        _kv_proj_kernel,
        grid=(t // tm,),
        in_specs=[pl.BlockSpec((tm, d), row), pl.BlockSpec((1, d), const), pl.BlockSpec((d, n), const),
                  pl.BlockSpec((tm, LANES), row), pl.BlockSpec((tm, LANES), row)],
        out_specs=pl.BlockSpec((tm, n), row),
        out_shape=jax.ShapeDtypeStruct((t, n), F32),
        compiler_params=_cparams(("parallel",)),
        name="kv_proj",
    )(x, g.reshape(1, d), w, cosp, sinp)


def _level_masks():
    ri = lax.broadcasted_iota(jnp.int32, (CHUNK, CHUNK), 0)
    ci = lax.broadcasted_iota(jnp.int32, (CHUNK, CHUNK), 1)
    masks = []
    for l in range(1, 7):
        same = (ri >> l) == (ci >> l)
        diff_half = (ri >> (l - 1)) != (ci >> (l - 1))
        masks.append((same & diff_half & (ri > ci)).astype(F32))
    return ri, ci, masks


def _bmm(a, b):
    return jnp.einsum("gmk,gkn->gmn", a, b, preferred_element_type=F32)


def _unit_lower_inverse(lmat, ri, ci, masks):
    eye = (ri == ci).astype(F32)
    p = eye - lmat * masks[0]
    for l in range(1, 6):
        e = (lmat * masks[l]).astype(BF16)
        pb = p.astype(BF16)
        pe = _bmm(pb, e)
        p = p - _bmm(pe.astype(BF16), pb)
    return p


def _gdn_kernel(x0_ref, xn_ref, gx_ref, w_ref, cw_ref, alog_ref, dtb_ref, ng_ref, o_ref, mq_ref,
                state_ref, carry_ref, proj_ref, next_ref, *, rows):
    t = pl.program_id(1)
    nc = rows // CHUNK
    nh = GDN_HEADS
    hw = nh * GDN_D
    mw = MEM_HEADS * MEM_DH

    def project(x_ref):
        hx = _rms(x_ref[...], gx_ref[...]).astype(BF16)
        return jnp.dot(hx, w_ref[...], preferred_element_type=F32)

    @pl.when(t == 0)
    def _():
        state_ref[...] = jnp.zeros_like(state_ref)
        carry_ref[...] = jnp.zeros_like(carry_ref)
        proj_ref[...] = project(x0_ref)

    next_ref[...] = project(xn_ref)
    qkv_ref = proj_ref.at[:, :3 * hw]
    z_ref = proj_ref.at[:, 3 * hw:4 * hw]
    ba_ref = proj_ref.at[:, 4 * hw + mw:]
    mq_ref[...] = proj_ref[:, 4 * hw:4 * hw + mw]

    ri, ci, masks = _level_masks()
    tril = ri >= ci
    strict = (ri > ci).astype(F32)

    ba = ba_ref[...]
    beta_all = _sigmoid(ba)
    sp_in = ba + dtb_ref[...]
    softplus = jnp.maximum(sp_in, 0.0) + jnp.log(1.0 + jnp.exp(-jnp.abs(sp_in)))
    g_all = -jnp.exp(alog_ref[...]) * softplus
    ti = lax.broadcasted_iota(jnp.int32, (rows, rows), 0)
    tj = lax.broadcasted_iota(jnp.int32, (rows, rows), 1)
    tri = (((ti // CHUNK) == (tj // CHUNK)) & (tj <= ti)).astype(BF16)
    g_hi = g_all.astype(BF16)
    rem = g_all - g_hi.astype(F32)
    g_mid = rem.astype(BF16)
    g_lo = (rem - g_mid.astype(F32)).astype(BF16)
    gsum = jnp.dot(tri, jnp.concatenate([g_hi, g_mid, g_lo], axis=1), preferred_element_type=F32)
    gc_all = gsum[:, :LANES] + gsum[:, LANES:2 * LANES] + gsum[:, 2 * LANES:]
    gc_t = jnp.transpose(gc_all)

    row8 = lax.broadcasted_iota(jnp.int32, (8, GDN_D), 0)

    def conv_silu(col):
        x = qkv_ref[:, col:col + GDN_D]
        c8 = carry_ref[:, col:col + GDN_D]
        acc = x * cw_ref[CONV_K - 1:CONV_K, col:col + GDN_D]
        for k in range(1, CONV_K):
            xr = pltpu.roll(x, k, axis=0)
            cr = pltpu.roll(c8, k, axis=0)
            head = jnp.where(row8 < k, cr, xr[:8])
            sh = jnp.concatenate([head, xr[8:]], axis=0)
            acc = acc + sh * cw_ref[CONV_K - 1 - k:CONV_K - k, col:col + GDN_D]
        return _silu_of_half(acc)

    parts = {name: [] for name in ("q", "k", "kb", "rhs", "qg", "kg", "gcb", "gcrow", "glast")}
    for h in range(nh):
        q = conv_silu(h * GDN_D)
        k = conv_silu(hw + h * GDN_D)
        v = conv_silu(2 * hw + h * GDN_D)
        q = q * lax.rsqrt(jnp.sum(q * q, axis=-1, keepdims=True) + EPS) * (GDN_D ** -0.5)
        k = k * lax.rsqrt(jnp.sum(k * k, axis=-1, keepdims=True) + EPS)
        beta = jnp.broadcast_to(beta_all[:, h:h + 1], (rows, GDN_D))
        gc = jnp.broadcast_to(gc_all[:, nh + h:nh + h + 1], (rows, GDN_D))
        gc3 = gc.reshape(nc, CHUNK, GDN_D)
        glast = jnp.broadcast_to(gc3[:, CHUNK - 1:CHUNK, :], (nc, CHUNK, GDN_D))
        eg = jnp.exp(gc)
        kb = k * beta
        three = lambda a: a.reshape(nc, CHUNK, a.shape[-1])
        parts["q"].append(three(q.astype(BF16)))
        parts["k"].append(three(k.astype(BF16)))
        parts["kb"].append(three(kb.astype(BF16)))
        parts["rhs"].append(three(jnp.concatenate([v * beta, kb * eg], axis=1).astype(BF16)))
        parts["qg"].append(three((q * eg).astype(BF16)))
        parts["kg"].append(three(k) * jnp.exp(glast - gc3))
        parts["gcb"].append(gc3[:, :, :CHUNK])
        parts["glast"].append(glast[:, :1, :])
        for c in range(nc):
            parts["gcrow"].append(
                gc_t[nh + h:nh + h + 1, c * CHUNK:(c + 1) * CHUNK].reshape(1, 1, CHUNK))
    st = {name: jnp.concatenate(v, axis=0) for name, v in parts.items()}

    decay = jnp.exp(jnp.where(tril, st["gcb"] - st["gcrow"], -jnp.inf))
    kq = jnp.einsum("gmd,gnd->gmn", jnp.concatenate([st["kb"], st["q"]], axis=1), st["k"],
                    preferred_element_type=F32)
    lmat = kq[:, :CHUNK] * (decay * strict)
    intra = (kq[:, CHUNK:] * decay).astype(BF16)
    tinv = _unit_lower_inverse(lmat, ri, ci, masks)
    sol = _bmm(tinv.astype(BF16), st["rhs"])

    four = lambda a: a.reshape((nh, nc) + a.shape[1:])
    u4 = four(sol[:, :, :GDN_D])
    w4 = four(sol[:, :, GDN_D:].astype(BF16))
    qg4, kg4, intra4 = four(st["qg"]), four(st["kg"]), four(intra)
    egl4 = four(jnp.exp(st["glast"]))
    zg = _silu_of_half(z_ref[...])
    state = state_ref[...]
    for c in range(nc):
        wq = jnp.concatenate([w4[:, c], qg4[:, c]], axis=1)
        ws_qs = _bmm(wq, state.astype(BF16))
        v_new = (u4[:, c] - ws_qs[:, :CHUNK]).astype(BF16)
        o_c = ws_qs[:, CHUNK:] + _bmm(intra4[:, c], v_new)
        kgt = jnp.stack([jnp.transpose(kg4[h, c]) for h in range(nh)], axis=0).astype(BF16)
        state = state * egl4[:, c] + _bmm(kgt, v_new)
        sl = slice(c * CHUNK, (c + 1) * CHUNK)
        for h in range(nh):
            cs = slice(h * GDN_D, (h + 1) * GDN_D)
            o_ref[sl, cs] = _rms(o_c[h], ng_ref[...]) * zg[sl, cs]
    state_ref[...] = state
    carry_ref[...] = qkv_ref[rows - 8:rows, :]
    proj_ref[...] = next_ref[...]


def gdn(x, gx, w, conv_w, a_log_p, dt_bias_p, norm_g, *, batch, seq, rows):
    t, d = x.shape
    nt = seq // rows
    hw = GDN_HEADS * GDN_D
    mw = MEM_HEADS * MEM_DH
    n = w.shape[1]
    kern = functools.partial(_gdn_kernel, rows=rows)
    const = lambda b, i: (0, 0)
    row = lambda b, i: (b * nt + i, 0)
    return pl.pallas_call(
        kern,
        grid=(batch, nt),
        in_specs=[pl.BlockSpec((rows, d), lambda b, i: (b * nt, 0)),
                  pl.BlockSpec((rows, d), lambda b, i: (b * nt + jnp.minimum(i + 1, nt - 1), 0)),
                  pl.BlockSpec((1, d), const),
                  pl.BlockSpec((d, n), const, pipeline_mode=pl.Buffered(1)),
                  pl.BlockSpec((CONV_K, 3 * hw), const),
                  pl.BlockSpec((1, LANES), const),
                  pl.BlockSpec((1, LANES), const),
                  pl.BlockSpec((1, GDN_D), const)],
        out_specs=[pl.BlockSpec((rows, hw), row), pl.BlockSpec((rows, mw), row)],
        out_shape=[jax.ShapeDtypeStruct((t, hw), F32), jax.ShapeDtypeStruct((t, mw), F32)],
        scratch_shapes=[pltpu.VMEM((GDN_HEADS, GDN_D, GDN_D), F32),
                        pltpu.VMEM((8, 3 * hw), F32),
                        pltpu.VMEM((rows, n), F32),
                        pltpu.VMEM((rows, n), F32)],
        compiler_params=_cparams(("arbitrary", "arbitrary")),
        name="gdn",
    )(x, x, gx.reshape(1, d), w, conv_w, a_log_p, dt_bias_p, norm_g.reshape(1, GDN_D))


def _swa_kernel(sink_ref, x_ref, g_ref, wq_ref, cos_ref, sin_ref, kvc_ref, kvp_ref, o_ref, mq_ref, q_ref,
                *, qrows):
    t = pl.program_id(1)
    qw = SWA_HEADS * SWA_DH
    hx = _rms(x_ref[...], g_ref[...]).astype(BF16)
    q_ref[...] = jnp.dot(hx, wq_ref[...], preferred_element_type=F32)
    mq_ref[...] = q_ref[:, qw:]
    nsub = qrows // SWA_BLOCK
    lane = lax.broadcasted_iota(jnp.int32, (1, LANES), 1)
    lo = lane < SWA_DH
    qi = lax.broadcasted_iota(jnp.int32, (SWA_BLOCK, 2 * SWA_BLOCK), 0)
    ki = lax.broadcasted_iota(jnp.int32, (SWA_BLOCK, 2 * SWA_BLOCK), 1)
    band = (ki > qi) & (ki <= qi + SWA_BLOCK)
    band_first = band & (ki >= jnp.where(t > 0, 0, SWA_BLOCK))
    dn = (((1,), (1,)), ((), ()))
    pairs_per_kv = SWA_HEADS // SWA_KV_HEADS // 2
    ones = jnp.ones((2 * SWA_BLOCK, LANES), BF16)

    kv_all = jnp.concatenate([kvp_ref[...], kvc_ref[...]], axis=0)
    kblk = kv_all[:, :LANES]
    vblk = kv_all[:, LANES:]
    kroll = pltpu.roll(kblk, SWA_DH, axis=1)
    vroll = pltpu.roll(vblk, SWA_DH, axis=1)
    kexp = [(jnp.where(lo, kblk, 0.0).astype(BF16), jnp.where(lo, 0.0, kroll).astype(BF16)),
            (jnp.where(lo, kroll, 0.0).astype(BF16), jnp.where(lo, 0.0, kblk).astype(BF16))]
    vexp = [(jnp.where(lo, vblk, 0.0).astype(BF16), jnp.where(lo, 0.0, vroll).astype(BF16)),
            (jnp.where(lo, vroll, 0.0).astype(BF16), jnp.where(lo, 0.0, vblk).astype(BF16))]

    for j in range(nsub):
        ws = slice(j * SWA_BLOCK, (j + 2) * SWA_BLOCK)
        rs = slice(j * SWA_BLOCK, (j + 1) * SWA_BLOCK)
        mask = band_first if j == 0 else band
        cosp, sinp = cos_ref[rs, :], sin_ref[rs, :]
        groups = [(g, half) for g in range(SWA_KV_HEADS) for half in range(2)]
        pairs = lambda g: [g * pairs_per_kv + i for i in range(pairs_per_kv)]
        qs = [jnp.concatenate(
            [(_rope(q_ref[rs, p * LANES:(p + 1) * LANES], cosp, sinp) * (SWA_DH ** -0.5)).astype(BF16)
             for p in pairs(g)], axis=0) for g in range(SWA_KV_HEADS)]
        scs = [lax.dot_general(qs[g], kexp[g][half][ws], dn, preferred_element_type=F32)
               for g, half in groups]
        es, sink_terms = [], []
        for (g, half), sc in zip(groups, scs):
            e_g, s_g = [], []
            for i, p in enumerate(pairs(g)):
                sink = sink_ref[2 * p + half]
                s_i = jnp.where(mask, sc[i * SWA_BLOCK:(i + 1) * SWA_BLOCK], -jnp.inf)
                m = jnp.maximum(jnp.max(s_i, axis=-1, keepdims=True), sink)
                e_g.append(jnp.exp(s_i - m).astype(BF16))
                s_g.append(jnp.broadcast_to(jnp.exp(sink - m), (SWA_BLOCK, LANES)))
            es.append(jnp.concatenate(e_g, axis=0))
            sink_terms.append(jnp.concatenate(s_g, axis=0))
        pvs = []
        for n, (g, half) in enumerate(groups):
            r = jnp.dot(es[n], jnp.concatenate([vexp[g][half][ws], ones], axis=1),
                        preferred_element_type=F32)
            pvs.append(r[:, :LANES] / (r[:, LANES:] + sink_terms[n]))
        for g in range(SWA_KV_HEADS):
            acc = pvs[2 * g] + pvs[2 * g + 1]
            for i, p in enumerate(pairs(g)):
                o_ref[rs, p * LANES:(p + 1) * LANES] = acc[i * SWA_BLOCK:(i + 1) * SWA_BLOCK]


def swa(x, g, wq, layer, kv, sinks, cosp, sinp, *, batch, seq, qrows):
    t, d = x.shape
    nt = seq // qrows
    qw = SWA_HEADS * SWA_DH
    n = wq.shape[-1]
    nsub = qrows // SWA_BLOCK
    nblk = seq // SWA_BLOCK
    row = lambda b, i: (b * nt + i, 0)
    const = lambda b, i: (0, 0)
    kern = functools.partial(_swa_kernel, qrows=qrows)
    return pl.pallas_call(
        kern,
        grid=(batch, nt),
        in_specs=[pl.BlockSpec(memory_space=pltpu.SMEM),
                  pl.BlockSpec((qrows, d), row),
                  pl.BlockSpec((1, d), const),
                  pl.BlockSpec((None, d, n), lambda b, i: (layer, 0, 0)),
                  pl.BlockSpec((qrows, LANES), row), pl.BlockSpec((qrows, LANES), row),
                  pl.BlockSpec((qrows, 2 * LANES), row),
                  pl.BlockSpec((SWA_BLOCK, 2 * LANES),
                               lambda b, i: (b * nblk + jnp.maximum(i * nsub - 1, 0), 0))],
        out_specs=[pl.BlockSpec((qrows, qw), row), pl.BlockSpec((qrows, n - qw), row)],
        out_shape=[jax.ShapeDtypeStruct((t, qw), F32), jax.ShapeDtypeStruct((t, n - qw), F32)],
        scratch_shapes=[pltpu.VMEM((qrows, n), F32)],
        compiler_params=_cparams(("parallel", "arbitrary")),
        name="swa",
    )(sinks, x, g.reshape(1, d), wq, cosp, sinp, kv, kv)


def _mix_ffn_kernel(x_ref, mix_ref, mq_ref, mkv_ref, wo_ref, g_ref, wg_ref, wu_ref, wd_ref, gf_ref,
                    o_ref, acc_ref, *, ff_tile, final):
    lane = lax.broadcasted_iota(jnp.int32, (1, LANES), 1)
    lo = lane < MEM_DH
    mw = MEM_HEADS * MEM_DH
    dn = (((1,), (1,)), ((), ()))
    outs = [mix_ref[...].astype(BF16)]
    for p in range(MEM_HEADS // 2):
        qp = (mq_ref[:, p * LANES:(p + 1) * LANES] * (MEM_DH ** -0.5)).astype(BF16)
        kp = mkv_ref[:, p * LANES:(p + 1) * LANES]
        vp = mkv_ref[:, mw + p * LANES:mw + (p + 1) * LANES]
        acc = None
        for half in range(2):
            sel = lo if half == 0 else jnp.logical_not(lo)
            kh = jnp.where(sel, kp, 0.0).astype(BF16)
            vh = jnp.where(sel, vp, 0.0).astype(BF16)
            sc = lax.dot_general(qp, kh, dn, preferred_element_type=F32)
            m = jnp.max(sc, axis=-1, keepdims=True)
            e = jnp.exp(sc - m)
            scale = jnp.broadcast_to(1.0 / jnp.sum(e, axis=-1, keepdims=True), (e.shape[0], LANES))
            pv = jnp.dot(e.astype(BF16), vh, preferred_element_type=F32) * scale
            acc = pv if acc is None else acc + pv
        outs.append(acc.astype(BF16))
    cat = jnp.concatenate(outs, axis=1)
    x = x_ref[...] + jnp.dot(cat, wo_ref[...], preferred_element_type=F32)
    h = _rms(x, g_ref[...]).astype(BF16)
    d_ff = wg_ref.shape[1]
    acc_ref[...] = x
    for j in range(d_ff // ff_tile):
        cs = slice(j * ff_tile, (j + 1) * ff_tile)
        gate = jnp.dot(h, wg_ref[:, cs], preferred_element_type=F32)
        up = jnp.dot(h, wu_ref[:, cs], preferred_element_type=F32)
        act = (_silu(gate) * up).astype(BF16)
        acc_ref[...] += jnp.dot(act, wd_ref[cs, :], preferred_element_type=F32)
    y = acc_ref[...]
    o_ref[...] = _rms(y, gf_ref[...]) if final else y


def mix_ffn(x, mix, mem_q, mkv, w_out, g, wgu, wd, g_final, *, layer, batch, seq, tm, ff_tile, final):
    t, d = x.shape
    nt = seq // tm
    mixw = mix.shape[1]
    mw = MEM_HEADS * MEM_DH
    mlen = mkv.shape[0] // batch
    d_ff = wd.shape[1]
    row = lambda b, i: (b * nt + i, 0)
    const = lambda b, i: (0, 0)
    single = pl.Buffered(1)
    kern = functools.partial(_mix_ffn_kernel, ff_tile=ff_tile, final=final)
    return pl.pallas_call(
        kern,
        grid=(batch, nt),
        in_specs=[pl.BlockSpec((tm, d), row),
                  pl.BlockSpec((tm, mixw), row),
                  pl.BlockSpec((tm, mw), row),
                  pl.BlockSpec((mlen, 2 * mw), lambda b, i: (b, 0)),
                  pl.BlockSpec((None, mixw + mw, d), lambda b, i: (layer, 0, 0), pipeline_mode=single),
                  pl.BlockSpec((1, d), const),
                  pl.BlockSpec((None, d, d_ff), lambda b, i: (layer, 0, 0), pipeline_mode=single),
                  pl.BlockSpec((None, d, d_ff), lambda b, i: (layer, 0, 1), pipeline_mode=single),
                  pl.BlockSpec((None, d_ff, d), lambda b, i: (layer, 0, 0), pipeline_mode=single),
                  pl.BlockSpec((1, d), const)],
        out_specs=pl.BlockSpec((tm, d), row),
        out_shape=jax.ShapeDtypeStruct((t, d), F32),
        scratch_shapes=[pltpu.VMEM((tm, d), F32)],
        compiler_params=_cparams(("parallel", "parallel")),
        name="mix_ffn",
    )(x, mix, mem_q, mkv, w_out, g.reshape(1, d), wgu, wgu, wd, g_final.reshape(1, d))


def _rope_table_kernel(pos_ref, inv_ref, cos_ref, sin_ref):
    half = ROT_DIM // 2
    lane = lax.broadcasted_iota(jnp.int32, (1, LANES), 1) & (SWA_DH - 1)
    for r in range(pos_ref.shape[0]):
        prow = pos_ref[r:r + 1, :].astype(F32)
        pcol = jnp.transpose(jnp.broadcast_to(prow, (LANES, LANES)))
        ang = pcol * inv_ref[...]
        sin = jnp.sin(ang)
        rs = slice(r * LANES, (r + 1) * LANES)
        cos_ref[rs, :] = jnp.where(lane < ROT_DIM, jnp.cos(ang), 1.0)
        sin_ref[rs, :] = jnp.where(lane < half, -sin, jnp.where(lane < ROT_DIM, sin, 0.0))


def rope_tables(positions):
    half = ROT_DIM // 2
    t = positions.size
    rows = 8
    inv = ROPE_THETA ** (-jnp.arange(0, ROT_DIM, 2, dtype=F32) / ROT_DIM)
    inv_lane = inv[(jnp.arange(LANES) % SWA_DH) % half].reshape(1, LANES)
    out = jax.ShapeDtypeStruct((t, LANES), F32)
    return pl.pallas_call(
        _rope_table_kernel,
        grid=(t // (rows * LANES),),
        in_specs=[pl.BlockSpec((rows, LANES), lambda i: (i, 0)), pl.BlockSpec((1, LANES), lambda i: (0, 0))],
        out_specs=[pl.BlockSpec((rows * LANES, LANES), lambda i: (i, 0))] * 2,
        out_shape=[out, out],
        compiler_params=_cparams(("parallel",)),
        name="rope_tables",
    )(positions.reshape(t // LANES, LANES), inv_lane)


def _pad_lanes(v, offset):
    out = jnp.zeros((1, LANES), F32)
    return lax.dynamic_update_slice(out, v.reshape(1, -1).astype(F32), (0, offset))


def kernel(x, mem, positions, ln_mix, ln_ffn, ln_mem, w_mem_kv, w_out, w_gate_up, w_down,
           gdn_w_in, gdn_conv, gdn_A_log, gdn_dt_bias, gdn_norm,
           swa_w_q, swa_sinks, ln_kv, w_kv, ln_final):
    batch, seq, d = x.shape
    depth = ln_mix.shape[0]
    n_a = gdn_w_in.shape[0]
    t = batch * seq
    mlen = mem.shape[1]
    hw = GDN_HEADS * GDN_D

    cosp, sinp = rope_tables(positions)
    xs = x.reshape(t, d)
    mem2 = mem.reshape(batch * mlen, d)
    w_mem_kv, w_out, w_gate_up, w_down, swa_w_q = (
        w.astype(BF16) for w in (w_mem_kv, w_out, w_gate_up, w_down, swa_w_q))
    kv = None
    for layer in range(depth):
        mkv = norm_matmul(mem2, ln_mem, w_mem_kv, layer=layer, tm=batch * mlen)
        if layer < n_a:
            a = layer
            o2 = 4 * hw
            w_r = gdn_weight_prep(gdn_w_in, a, o2=o2, n_gate=2 * GDN_HEADS, tr=256)
            mix, proj = gdn(xs, ln_mix[layer], w_r, 0.5 * gdn_conv[a], _pad_lanes(gdn_A_log[a], GDN_HEADS),
                            _pad_lanes(gdn_dt_bias[a], GDN_HEADS), gdn_norm[a],
                            batch=batch, seq=seq, rows=256)
        else:
            bl = layer - n_a
            mix, proj = swa(xs, ln_mix[layer], swa_w_q, bl, kv, swa_sinks[bl], cosp, sinp,
                            batch=batch, seq=seq, qrows=512)
        xs = mix_ffn(xs, mix, proj, mkv, w_out, ln_ffn[layer], w_gate_up, w_down, ln_final,
                     layer=layer, batch=batch, seq=seq, tm=512, ff_tile=256, final=(layer == depth - 1))
        if layer == n_a - 1:
            kv = kv_proj(xs, ln_kv, w_kv.astype(BF16), cosp, sinp, tm=512)
    return xs.reshape(batch, seq, d)
```

```python
import functools

import jax
import jax.numpy as jnp
from jax import lax
from jax.experimental import pallas as pl
from jax.experimental.pallas import tpu as pltpu

F32 = jnp.float32
BF16 = jnp.bfloat16

EPS = 1e-6
LANES = 128
CHUNK = 64
CONV_K = 4
GDN_HEADS = 6
GDN_D = 128
SWA_HEADS = 12
SWA_KV_HEADS = 2
SWA_DH = 64
SWA_BLOCK = 128
ROT_DIM = 16
ROPE_THETA = 500000.0
MEM_HEADS = 4
MEM_DH = 64
VMEM_LIMIT = 56 * 1024 * 1024


def _cparams(sem):
    return pltpu.CompilerParams(dimension_semantics=sem, vmem_limit_bytes=VMEM_LIMIT)


def _rms(x, g):
    ms = jnp.mean(x * x, axis=-1, keepdims=True)
    return x * lax.rsqrt(ms + EPS) * g


def _sigmoid(x):
    return 0.5 + 0.5 * jnp.tanh(0.5 * x)


def _silu_of_half(h):
    return h + h * jnp.tanh(h)


def _silu(x):
    return _silu_of_half(0.5 * x)


def _rope(x, cosp, sinp):
    half = ROT_DIM // 2
    lane = lax.broadcasted_iota(jnp.int32, (1, LANES), 1) & (SWA_DH - 1)
    partner = jnp.where(lane < half, pltpu.roll(x, LANES - half, axis=1), pltpu.roll(x, half, axis=1))
    return x * cosp + partner * sinp


def _norm_matmul_kernel(x_ref, g_ref, w_ref, o_ref):
    h = _rms(x_ref[...], g_ref[...]).astype(BF16)
    o_ref[...] = jnp.dot(h, w_ref[...], preferred_element_type=F32)


def norm_matmul(x, g, w, *, tm, layer=None):
    t, d = x.shape
    n = w.shape[-1]
    if layer is None:
        w_spec = pl.BlockSpec((d, n), lambda i: (0, 0))
    else:
        w_spec = pl.BlockSpec((None, d, n), lambda i: (layer, 0, 0))
    return pl.pallas_call(
        _norm_matmul_kernel,
        grid=(t // tm,),
        in_specs=[pl.BlockSpec((tm, d), lambda i: (i, 0)), pl.BlockSpec((1, d), lambda i: (0, 0)), w_spec],
        out_specs=pl.BlockSpec((tm, n), lambda i: (i, 0)),
        out_shape=jax.ShapeDtypeStruct((t, n), F32),
        compiler_params=_cparams(("parallel",)),
        name="norm_matmul",
    )(x, g.reshape(1, d), w)


def _gdn_weight_prep_kernel(w_ref, o_ref, *, o2, n_gate):
    w = w_ref[...]
    n_in = w.shape[1]
    mq = n_in - o2 - n_gate
    zc = 3 * o2 // 4
    o_ref[:, :zc] = w[:, :zc].astype(BF16)
    o_ref[:, zc:o2] = (0.5 * w[:, zc:o2]).astype(BF16)
    o_ref[:, o2:o2 + mq] = w[:, o2 + n_gate:].astype(BF16)
    gates = jnp.concatenate([w[:, o2:o2 + n_gate], jnp.zeros((w.shape[0], LANES - n_gate), F32)], axis=1)
    o_ref[:, o2 + mq:] = gates.astype(BF16)


def gdn_weight_prep(w_in_all, layer, *, o2, n_gate, tr):
    _, d, n_in = w_in_all.shape
    n_out = n_in - n_gate + LANES
    kern = functools.partial(_gdn_weight_prep_kernel, o2=o2, n_gate=n_gate)
    return pl.pallas_call(
        kern,
        grid=(d // tr,),
        in_specs=[pl.BlockSpec((None, tr, n_in), lambda i: (layer, i, 0))],
        out_specs=pl.BlockSpec((tr, n_out), lambda i: (i, 0)),
        out_shape=jax.ShapeDtypeStruct((d, n_out), BF16),
        compiler_params=_cparams(("parallel",)),
        name="gdn_weight_prep",
    )(w_in_all)


def _kv_proj_kernel(x_ref, g_ref, w_ref, cos_ref, sin_ref, o_ref):
    h = _rms(x_ref[...], g_ref[...]).astype(BF16)
    kv = jnp.dot(h, w_ref[...], preferred_element_type=F32)
    o_ref[:, :LANES] = _rope(kv[:, :LANES], cos_ref[...], sin_ref[...])
    o_ref[:, LANES:] = kv[:, LANES:]


def kv_proj(x, g, w, cosp, sinp, *, tm):
    t, d = x.shape
    n = w.shape[1]
    row = lambda i: (i, 0)
    const = lambda i: (0, 0)
    return pl.pallas_call(
        _kv_proj_kernel,
        grid=(t // tm,),
        in_specs=[pl.BlockSpec((tm, d), row), pl.BlockSpec((1, d), const), pl.BlockSpec((d, n), const),
                  pl.BlockSpec((tm, LANES), row), pl.BlockSpec((tm, LANES), row)],
        out_specs=pl.BlockSpec((tm, n), row),
        out_shape=jax.ShapeDtypeStruct((t, n), F32),
        compiler_params=_cparams(("parallel",)),
        name="kv_proj",
    )(x, g.reshape(1, d), w, cosp, sinp)


def _level_masks():
    ri = lax.broadcasted_iota(jnp.int32, (CHUNK, CHUNK), 0)
    ci = lax.broadcasted_iota(jnp.int32, (CHUNK, CHUNK), 1)
    masks = []
    for l in range(1, 7):
        same = (ri >> l) == (ci >> l)
        diff_half = (ri >> (l - 1)) != (ci >> (l - 1))
        masks.append((same & diff_half & (ri > ci)).astype(F32))
    return ri, ci, masks


def _bmm(a, b):
    return jnp.einsum("gmk,gkn->gmn", a, b, preferred_element_type=F32)


def _unit_lower_inverse(lmat, ri, ci, masks):
    eye = (ri == ci).astype(F32)
    p = eye - lmat * masks[0]
    for l in range(1, 6):
        e = (lmat * masks[l]).astype(BF16)
        pb = p.astype(BF16)
        pe = _bmm(pb, e)
        p = p - _bmm(pe.astype(BF16), pb)
    return p


def _gdn_kernel(x0_ref, xn_ref, gx_ref, w_ref, cw_ref, alog_ref, dtb_ref, ng_ref, o_ref, mq_ref,
                state_ref, carry_ref, proj_ref, next_ref, *, rows):
    t = pl.program_id(1)
    nc = rows // CHUNK
    nh = GDN_HEADS
    hw = nh * GDN_D
    mw = MEM_HEADS * MEM_DH

    def project(x_ref):
        hx = _rms(x_ref[...], gx_ref[...]).astype(BF16)
        return jnp.dot(hx, w_ref[...], preferred_element_type=F32)

    @pl.when(t == 0)
    def _():
        state_ref[...] = jnp.zeros_like(state_ref)
        carry_ref[...] = jnp.zeros_like(carry_ref)
        proj_ref[...] = project(x0_ref)

    next_ref[...] = project(xn_ref)
    qkv_ref = proj_ref.at[:, :3 * hw]
    z_ref = proj_ref.at[:, 3 * hw:4 * hw]
    ba_ref = proj_ref.at[:, 4 * hw + mw:]
    mq_ref[...] = proj_ref[:, 4 * hw:4 * hw + mw]

    ri, ci, masks = _level_masks()
    tril = ri >= ci
    strict = (ri > ci).astype(F32)

    ba = ba_ref[...]
    beta_all = _sigmoid(ba)
    sp_in = ba + dtb_ref[...]
    softplus = jnp.maximum(sp_in, 0.0) + jnp.log(1.0 + jnp.exp(-jnp.abs(sp_in)))
    g_all = -jnp.exp(alog_ref[...]) * softplus
    ti = lax.broadcasted_iota(jnp.int32, (rows, rows), 0)
    tj = lax.broadcasted_iota(jnp.int32, (rows, rows), 1)
    tri = (((ti // CHUNK) == (tj // CHUNK)) & (tj <= ti)).astype(BF16)
    g_hi = g_all.astype(BF16)
    rem = g_all - g_hi.astype(F32)
    g_mid = rem.astype(BF16)
    g_lo = (rem - g_mid.astype(F32)).astype(BF16)
    gsum = jnp.dot(tri, jnp.concatenate([g_hi, g_mid, g_lo], axis=1), preferred_element_type=F32)
    gc_all = gsum[:, :LANES] + gsum[:, LANES:2 * LANES] + gsum[:, 2 * LANES:]
    gc_t = jnp.transpose(gc_all)

    row8 = lax.broadcasted_iota(jnp.int32, (8, GDN_D), 0)

    def conv_silu(col):
        x = qkv_ref[:, col:col + GDN_D]
        c8 = carry_ref[:, col:col + GDN_D]
        acc = x * cw_ref[CONV_K - 1:CONV_K, col:col + GDN_D]
        for k in range(1, CONV_K):
            xr = pltpu.roll(x, k, axis=0)
            cr = pltpu.roll(c8, k, axis=0)
            head = jnp.where(row8 < k, cr, xr[:8])
            sh = jnp.concatenate([head, xr[8:]], axis=0)
            acc = acc + sh * cw_ref[CONV_K - 1 - k:CONV_K - k, col:col + GDN_D]
        return _silu_of_half(acc)

    parts = {name: [] for name in ("q", "k", "kb", "rhs", "qg", "kg", "gcb", "gcrow", "glast")}
    for h in range(nh):
        q = conv_silu(h * GDN_D)
        k = conv_silu(hw + h * GDN_D)
        v = conv_silu(2 * hw + h * GDN_D)
        q = q * lax.rsqrt(jnp.sum(q * q, axis=-1, keepdims=True) + EPS) * (GDN_D ** -0.5)
        k = k * lax.rsqrt(jnp.sum(k * k, axis=-1, keepdims=True) + EPS)
        beta = jnp.broadcast_to(beta_all[:, h:h + 1], (rows, GDN_D))
        gc = jnp.broadcast_to(gc_all[:, nh + h:nh + h + 1], (rows, GDN_D))
        gc3 = gc.reshape(nc, CHUNK, GDN_D)
        glast = jnp.broadcast_to(gc3[:, CHUNK - 1:CHUNK, :], (nc, CHUNK, GDN_D))
        eg = jnp.exp(gc)
        kb = k * beta
        three = lambda a: a.reshape(nc, CHUNK, a.shape[-1])
        parts["q"].append(three(q.astype(BF16)))
        parts["k"].append(three(k.astype(BF16)))
        parts["kb"].append(three(kb.astype(BF16)))
        parts["rhs"].append(three(jnp.concatenate([v * beta, kb * eg], axis=1).astype(BF16)))
        parts["qg"].append(three((q * eg).astype(BF16)))
        parts["kg"].append(three(k) * jnp.exp(glast - gc3))
        parts["gcb"].append(gc3[:, :, :CHUNK])
        parts["glast"].append(glast[:, :1, :])
        for c in range(nc):
            parts["gcrow"].append(
                gc_t[nh + h:nh + h + 1, c * CHUNK:(c + 1) * CHUNK].reshape(1, 1, CHUNK))
    st = {name: jnp.concatenate(v, axis=0) for name, v in parts.items()}

    decay = jnp.exp(jnp.where(tril, st["gcb"] - st["gcrow"], -jnp.inf))
    kq = jnp.einsum("gmd,gnd->gmn", jnp.concatenate([st["kb"], st["q"]], axis=1), st["k"],
                    preferred_element_type=F32)
    lmat = kq[:, :CHUNK] * (decay * strict)
    intra = (kq[:, CHUNK:] * decay).astype(BF16)
    tinv = _unit_lower_inverse(lmat, ri, ci, masks)
    sol = _bmm(tinv.astype(BF16), st["rhs"])

    four = lambda a: a.reshape((nh, nc) + a.shape[1:])
    u4 = four(sol[:, :, :GDN_D])
    w4 = four(sol[:, :, GDN_D:].astype(BF16))
    qg4, kg4, intra4 = four(st["qg"]), four(st["kg"]), four(intra)
    egl4 = four(jnp.exp(st["glast"]))
    zg = _silu_of_half(z_ref[...])
    state = state_ref[...]
    for c in range(nc):
        wq = jnp.concatenate([w4[:, c], qg4[:, c]], axis=1)
        ws_qs = _bmm(wq, state.astype(BF16))
        v_new = (u4[:, c] - ws_qs[:, :CHUNK]).astype(BF16)
        o_c = ws_qs[:, CHUNK:] + _bmm(intra4[:, c], v_new)
        kgt = jnp.stack([jnp.transpose(kg4[h, c]) for h in range(nh)], axis=0).astype(BF16)
        state = state * egl4[:, c] + _bmm(kgt, v_new)
        sl = slice(c * CHUNK, (c + 1) * CHUNK)
        for h in range(nh):
            cs = slice(h * GDN_D, (h + 1) * GDN_D)
            o_ref[sl, cs] = _rms(o_c[h], ng_ref[...]) * zg[sl, cs]
    state_ref[...] = state
    carry_ref[...] = qkv_ref[rows - 8:rows, :]
    proj_ref[...] = next_ref[...]


def gdn(x, gx, w, conv_w, a_log_p, dt_bias_p, norm_g, *, batch, seq, rows):
    t, d = x.shape
    nt = seq // rows
    hw = GDN_HEADS * GDN_D
    mw = MEM_HEADS * MEM_DH
    n = w.shape[1]
    kern = functools.partial(_gdn_kernel, rows=rows)
    const = lambda b, i: (0, 0)
    row = lambda b, i: (b * nt + i, 0)
    return pl.pallas_call(
        kern,
        grid=(batch, nt),
        in_specs=[pl.BlockSpec((rows, d), lambda b, i: (b * nt, 0)),
                  pl.BlockSpec((rows, d), lambda b, i: (b * nt + jnp.minimum(i + 1, nt - 1), 0)),
                  pl.BlockSpec((1, d), const),
                  pl.BlockSpec((d, n), const, pipeline_mode=pl.Buffered(1)),
                  pl.BlockSpec((CONV_K, 3 * hw), const),
                  pl.BlockSpec((1, LANES), const),
                  pl.BlockSpec((1, LANES), const),
                  pl.BlockSpec((1, GDN_D), const)],
        out_specs=[pl.BlockSpec((rows, hw), row), pl.BlockSpec((rows, mw), row)],
        out_shape=[jax.ShapeDtypeStruct((t, hw), F32), jax.ShapeDtypeStruct((t, mw), F32)],
        scratch_shapes=[pltpu.VMEM((GDN_HEADS, GDN_D, GDN_D), F32),
                        pltpu.VMEM((8, 3 * hw), F32),
                        pltpu.VMEM((rows, n), F32),
                        pltpu.VMEM((rows, n), F32)],
        compiler_params=_cparams(("arbitrary", "arbitrary")),
        name="gdn",
    )(x, x, gx.reshape(1, d), w, conv_w, a_log_p, dt_bias_p, norm_g.reshape(1, GDN_D))


def _swa_kernel(sink_ref, x_ref, g_ref, wq_ref, cos_ref, sin_ref, kvc_ref, kvp_ref, o_ref, mq_ref, q_ref,
                *, qrows):
    t = pl.program_id(1)
    qw = SWA_HEADS * SWA_DH
    hx = _rms(x_ref[...], g_ref[...]).astype(BF16)
    q_ref[...] = jnp.dot(hx, wq_ref[...], preferred_element_type=F32)
    mq_ref[...] = q_ref[:, qw:]
    nsub = qrows // SWA_BLOCK
    lane = lax.broadcasted_iota(jnp.int32, (1, LANES), 1)
    lo = lane < SWA_DH
    qi = lax.broadcasted_iota(jnp.int32, (SWA_BLOCK, 2 * SWA_BLOCK), 0)
    ki = lax.broadcasted_iota(jnp.int32, (SWA_BLOCK, 2 * SWA_BLOCK), 1)
    band = (ki > qi) & (ki <= qi + SWA_BLOCK)
    band_first = band & (ki >= jnp.where(t > 0, 0, SWA_BLOCK))
    dn = (((1,), (1,)), ((), ()))
    pairs_per_kv = SWA_HEADS // SWA_KV_HEADS // 2
    ones = jnp.ones((2 * SWA_BLOCK, LANES), BF16)

    kv_all = jnp.concatenate([kvp_ref[...], kvc_ref[...]], axis=0)
    kblk = kv_all[:, :LANES]
    vblk = kv_all[:, LANES:]
    kroll = pltpu.roll(kblk, SWA_DH, axis=1)
    vroll = pltpu.roll(vblk, SWA_DH, axis=1)
    kexp = [(jnp.where(lo, kblk, 0.0).astype(BF16), jnp.where(lo, 0.0, kroll).astype(BF16)),
            (jnp.where(lo, kroll, 0.0).astype(BF16), jnp.where(lo, 0.0, kblk).astype(BF16))]
    vexp = [(jnp.where(lo, vblk, 0.0).astype(BF16), jnp.where(lo, 0.0, vroll).astype(BF16)),
            (jnp.where(lo, vroll, 0.0).astype(BF16), jnp.where(lo, 0.0, vblk).astype(BF16))]

    for j in range(nsub):
        ws = slice(j * SWA_BLOCK, (j + 2) * SWA_BLOCK)
        rs = slice(j * SWA_BLOCK, (j + 1) * SWA_BLOCK)
        mask = band_first if j == 0 else band
        cosp, sinp = cos_ref[rs, :], sin_ref[rs, :]
        groups = [(g, half) for g in range(SWA_KV_HEADS) for half in range(2)]
        pairs = lambda g: [g * pairs_per_kv + i for i in range(pairs_per_kv)]
        qs = [jnp.concatenate(
            [(_rope(q_ref[rs, p * LANES:(p + 1) * LANES], cosp, sinp) * (SWA_DH ** -0.5)).astype(BF16)
             for p in pairs(g)], axis=0) for g in range(SWA_KV_HEADS)]
        scs = [lax.dot_general(qs[g], kexp[g][half][ws], dn, preferred_element_type=F32)
               for g, half in groups]
        es, sink_terms = [], []
        for (g, half), sc in zip(groups, scs):
            e_g, s_g = [], []
            for i, p in enumerate(pairs(g)):
                sink = sink_ref[2 * p + half]
                s_i = jnp.where(mask, sc[i * SWA_BLOCK:(i + 1) * SWA_BLOCK], -jnp.inf)
                m = jnp.maximum(jnp.max(s_i, axis=-1, keepdims=True), sink)
                e_g.append(jnp.exp(s_i - m).astype(BF16))
                s_g.append(jnp.broadcast_to(jnp.exp(sink - m), (SWA_BLOCK, LANES)))
            es.append(jnp.concatenate(e_g, axis=0))
            sink_terms.append(jnp.concatenate(s_g, axis=0))
        pvs = []
        for n, (g, half) in enumerate(groups):
            r = jnp.dot(es[n], jnp.concatenate([vexp[g][half][ws], ones], axis=1),
                        preferred_element_type=F32)
            pvs.append(r[:, :LANES] / (r[:, LANES:] + sink_terms[n]))
        for g in range(SWA_KV_HEADS):
            acc = pvs[2 * g] + pvs[2 * g + 1]
            for i, p in enumerate(pairs(g)):
                o_ref[rs, p * LANES:(p + 1) * LANES] = acc[i * SWA_BLOCK:(i + 1) * SWA_BLOCK]


def swa(x, g, wq, layer, kv, sinks, cosp, sinp, *, batch, seq, qrows):
    t, d = x.shape
    nt = seq // qrows
    qw = SWA_HEADS * SWA_DH
    n = wq.shape[-1]
    nsub = qrows // SWA_BLOCK
    nblk = seq // SWA_BLOCK
    row = lambda b, i: (b * nt + i, 0)
    const = lambda b, i: (0, 0)
    kern = functools.partial(_swa_kernel, qrows=qrows)
    return pl.pallas_call(
        kern,
        grid=(batch, nt),
        in_specs=[pl.BlockSpec(memory_space=pltpu.SMEM),
                  pl.BlockSpec((qrows, d), row),
                  pl.BlockSpec((1, d), const),
                  pl.BlockSpec((None, d, n), lambda b, i: (layer, 0, 0)),
                  pl.BlockSpec((qrows, LANES), row), pl.BlockSpec((qrows, LANES), row),
                  pl.BlockSpec((qrows, 2 * LANES), row),
                  pl.BlockSpec((SWA_BLOCK, 2 * LANES),
                               lambda b, i: (b * nblk + jnp.maximum(i * nsub - 1, 0), 0))],
        out_specs=[pl.BlockSpec((qrows, qw), row), pl.BlockSpec((qrows, n - qw), row)],
        out_shape=[jax.ShapeDtypeStruct((t, qw), F32), jax.ShapeDtypeStruct((t, n - qw), F32)],
        scratch_shapes=[pltpu.VMEM((qrows, n), F32)],
        compiler_params=_cparams(("parallel", "arbitrary")),
        name="swa",
    )(sinks, x, g.reshape(1, d), wq, cosp, sinp, kv, kv)


def _mix_ffn_kernel(x_ref, mix_ref, mq_ref, mkv_ref, wo_ref, g_ref, wg_ref, wu_ref, wd_ref, gf_ref,
                    o_ref, acc_ref, *, ff_tile, final):
    lane = lax.broadcasted_iota(jnp.int32, (1, LANES), 1)
    lo = lane < MEM_DH
    mw = MEM_HEADS * MEM_DH
    dn = (((1,), (1,)), ((), ()))
    outs = [mix_ref[...].astype(BF16)]
    for p in range(MEM_HEADS // 2):
        qp = (mq_ref[:, p * LANES:(p + 1) * LANES] * (MEM_DH ** -0.5)).astype(BF16)
        kp = mkv_ref[:, p * LANES:(p + 1) * LANES]
        vp = mkv_ref[:, mw + p * LANES:mw + (p + 1) * LANES]
        acc = None
        for half in range(2):
            sel = lo if half == 0 else jnp.logical_not(lo)
            kh = jnp.where(sel, kp, 0.0).astype(BF16)
            vh = jnp.where(sel, vp, 0.0).astype(BF16)
            sc = lax.dot_general(qp, kh, dn, preferred_element_type=F32)
            m = jnp.max(sc, axis=-1, keepdims=True)
            e = jnp.exp(sc - m)
            scale = jnp.broadcast_to(1.0 / jnp.sum(e, axis=-1, keepdims=True), (e.shape[0], LANES))
            pv = jnp.dot(e.astype(BF16), vh, preferred_element_type=F32) * scale
            acc = pv if acc is None else acc + pv
        outs.append(acc.astype(BF16))
    cat = jnp.concatenate(outs, axis=1)
    x = x_ref[...] + jnp.dot(cat, wo_ref[...], preferred_element_type=F32)
    h = _rms(x, g_ref[...]).astype(BF16)
    d_ff = wg_ref.shape[1]
    acc_ref[...] = x
    for j in range(d_ff // ff_tile):
        cs = slice(j * ff_tile, (j + 1) * ff_tile)
        gate = jnp.dot(h, wg_ref[:, cs], preferred_element_type=F32)
        up = jnp.dot(h, wu_ref[:, cs], preferred_element_type=F32)
        act = (_silu(gate) * up).astype(BF16)
        acc_ref[...] += jnp.dot(act, wd_ref[cs, :], preferred_element_type=F32)
    y = acc_ref[...]
    o_ref[...] = _rms(y, gf_ref[...]) if final else y


def mix_ffn(x, mix, mem_q, mkv, w_out, g, wgu, wd, g_final, *, layer, batch, seq, tm, ff_tile, final):
    t, d = x.shape
    nt = seq // tm
    mixw = mix.shape[1]
    mw = MEM_HEADS * MEM_DH
    mlen = mkv.shape[0] // batch
    d_ff = wd.shape[1]
    row = lambda b, i: (b * nt + i, 0)
    const = lambda b, i: (0, 0)
    single = pl.Buffered(1)
    kern = functools.partial(_mix_ffn_kernel, ff_tile=ff_tile, final=final)
    return pl.pallas_call(
        kern,
        grid=(batch, nt),
        in_specs=[pl.BlockSpec((tm, d), row),
                  pl.BlockSpec((tm, mixw), row),
                  pl.BlockSpec((tm, mw), row),
                  pl.BlockSpec((mlen, 2 * mw), lambda b, i: (b, 0)),
                  pl.BlockSpec((None, mixw + mw, d), lambda b, i: (layer, 0, 0), pipeline_mode=single),
                  pl.BlockSpec((1, d), const),
                  pl.BlockSpec((None, d, d_ff), lambda b, i: (layer, 0, 0), pipeline_mode=single),
                  pl.BlockSpec((None, d, d_ff), lambda b, i: (layer, 0, 1), pipeline_mode=single),
                  pl.BlockSpec((None, d_ff, d), lambda b, i: (layer, 0, 0), pipeline_mode=single),
                  pl.BlockSpec((1, d), const)],
        out_specs=pl.BlockSpec((tm, d), row),
        out_shape=jax.ShapeDtypeStruct((t, d), F32),
        scratch_shapes=[pltpu.VMEM((tm, d), F32)],
        compiler_params=_cparams(("parallel", "parallel")),
        name="mix_ffn",
    )(x, mix, mem_q, mkv, w_out, g.reshape(1, d), wgu, wgu, wd, g_final.reshape(1, d))


def _rope_table_kernel(pos_ref, inv_ref, cos_ref, sin_ref):
    half = ROT_DIM // 2
    lane = lax.broadcasted_iota(jnp.int32, (1, LANES), 1) & (SWA_DH - 1)
    for r in range(pos_ref.shape[0]):
        prow = pos_ref[r:r + 1, :].astype(F32)
        pcol = jnp.transpose(jnp.broadcast_to(prow, (LANES, LANES)))
        ang = pcol * inv_ref[...]
        sin = jnp.sin(ang)
        rs = slice(r * LANES, (r + 1) * LANES)
        cos_ref[rs, :] = jnp.where(lane < ROT_DIM, jnp.cos(ang), 1.0)
        sin_ref[rs, :] = jnp.where(lane < half, -sin, jnp.where(lane < ROT_DIM, sin, 0.0))


def rope_tables(positions):
    half = ROT_DIM // 2
    t = positions.size
    rows = 8
    inv = ROPE_THETA ** (-jnp.arange(0, ROT_DIM, 2, dtype=F32) / ROT_DIM)
    inv_lane = inv[(jnp.arange(LANES) % SWA_DH) % half].reshape(1, LANES)
    out = jax.ShapeDtypeStruct((t, LANES), F32)
    return pl.pallas_call(
        _rope_table_kernel,
        grid=(t // (rows * LANES),),
        in_specs=[pl.BlockSpec((rows, LANES), lambda i: (i, 0)), pl.BlockSpec((1, LANES), lambda i: (0, 0))],
        out_specs=[pl.BlockSpec((rows * LANES, LANES), lambda i: (i, 0))] * 2,
        out_shape=[out, out],
        compiler_params=_cparams(("parallel",)),
        name="rope_tables",
    )(positions.reshape(t // LANES, LANES), inv_lane)


def _pad_lanes(v, offset):
    out = jnp.zeros((1, LANES), F32)
    return lax.dynamic_update_slice(out, v.reshape(1, -1).astype(F32), (0, offset))


def kernel(x, mem, positions, ln_mix, ln_ffn, ln_mem, w_mem_kv, w_out, w_gate_up, w_down,
           gdn_w_in, gdn_conv, gdn_A_log, gdn_dt_bias, gdn_norm,
           swa_w_q, swa_sinks, ln_kv, w_kv, ln_final):
    batch, seq, d = x.shape
    depth = ln_mix.shape[0]
    n_a = gdn_w_in.shape[0]
    t = batch * seq
    mlen = mem.shape[1]
    hw = GDN_HEADS * GDN_D

    cosp, sinp = rope_tables(positions)
    xs = x.reshape(t, d)
    mem2 = mem.reshape(batch * mlen, d)
    w_mem_kv, w_out, w_gate_up, w_down, swa_w_q = (
        w.astype(BF16) for w in (w_mem_kv, w_out, w_gate_up, w_down, swa_w_q))
    kv = None
    for layer in range(depth):
        mkv = norm_matmul(mem2, ln_mem, w_mem_kv, layer=layer, tm=batch * mlen)
        if layer < n_a:
            a = layer
            o2 = 4 * hw
            w_r = gdn_weight_prep(gdn_w_in, a, o2=o2, n_gate=2 * GDN_HEADS, tr=256)
            mix, proj = gdn(xs, ln_mix[layer], w_r, 0.5 * gdn_conv[a], _pad_lanes(gdn_A_log[a], GDN_HEADS),
                            _pad_lanes(gdn_dt_bias[a], GDN_HEADS), gdn_norm[a],
                            batch=batch, seq=seq, rows=256)
        else:
            bl = layer - n_a
            mix, proj = swa(xs, ln_mix[layer], swa_w_q, bl, kv, swa_sinks[bl], cosp, sinp,
                            batch=batch, seq=seq, qrows=512)
        xs = mix_ffn(xs, mix, proj, mkv, w_out, ln_ffn[layer], w_gate_up, w_down, ln_final,
                     layer=layer, batch=batch, seq=seq, tm=1024, ff_tile=256, final=(layer == depth - 1))
        if layer == n_a - 1:
            kv = kv_proj(xs, ln_kv, w_kv.astype(BF16), cosp, sinp, tm=512)
    return xs.reshape(batch, seq, d)
```

```python
import functools

import jax
import jax.numpy as jnp
from jax import lax
from jax.experimental import pallas as pl
from jax.experimental.pallas import tpu as pltpu

F32 = jnp.float32
BF16 = jnp.bfloat16

EPS = 1e-6
LANES = 128
CHUNK = 64
CONV_K = 4
GDN_HEADS = 6
GDN_D = 128
SWA_HEADS = 12
SWA_KV_HEADS = 2
SWA_DH = 64
SWA_BLOCK = 128
ROT_DIM = 16
ROPE_THETA = 500000.0
MEM_HEADS = 4
MEM_DH = 64
VMEM_LIMIT = 56 * 1024 * 1024


def _cparams(sem):
    return pltpu.CompilerParams(dimension_semantics=sem, vmem_limit_bytes=VMEM_LIMIT)


def _rms(x, g):
    ms = jnp.mean(x * x, axis=-1, keepdims=True)
    return x * lax.rsqrt(ms + EPS) * g


def _sigmoid(x):
    return 0.5 + 0.5 * jnp.tanh(0.5 * x)


def _silu_of_half(h):
    return h + h * jnp.tanh(h)


def _silu(x):
    return _silu_of_half(0.5 * x)


def _rope(x, cosp, sinp):
    half = ROT_DIM // 2
    lane = lax.broadcasted_iota(jnp.int32, (1, LANES), 1) & (SWA_DH - 1)
    partner = jnp.where(lane < half, pltpu.roll(x, LANES - half, axis=1), pltpu.roll(x, half, axis=1))
    return x * cosp + partner * sinp


def _norm_matmul_kernel(x_ref, g_ref, w_ref, o_ref):
    h = _rms(x_ref[...], g_ref[...]).astype(BF16)
    o_ref[...] = jnp.dot(h, w_ref[...], preferred_element_type=F32)


def norm_matmul(x, g, w, *, tm, layer=None):
    t, d = x.shape
    n = w.shape[-1]
    if layer is None:
        w_spec = pl.BlockSpec((d, n), lambda i: (0, 0))
    else:
        w_spec = pl.BlockSpec((None, d, n), lambda i: (layer, 0, 0))
    return pl.pallas_call(
        _norm_matmul_kernel,
        grid=(t // tm,),
        in_specs=[pl.BlockSpec((tm, d), lambda i: (i, 0)), pl.BlockSpec((1, d), lambda i: (0, 0)), w_spec],
        out_specs=pl.BlockSpec((tm, n), lambda i: (i, 0)),
        out_shape=jax.ShapeDtypeStruct((t, n), F32),
        compiler_params=_cparams(("parallel",)),
        name="norm_matmul",
    )(x, g.reshape(1, d), w)


def _gdn_weight_prep_kernel(w_ref, o_ref, *, o2, n_gate):
    w = w_ref[...]
    n_in = w.shape[1]
    mq = n_in - o2 - n_gate
    zc = 3 * o2 // 4
    o_ref[:, :zc] = w[:, :zc].astype(BF16)
    o_ref[:, zc:o2] = (0.5 * w[:, zc:o2]).astype(BF16)
    o_ref[:, o2:o2 + mq] = w[:, o2 + n_gate:].astype(BF16)
    gates = jnp.concatenate([w[:, o2:o2 + n_gate], jnp.zeros((w.shape[0], LANES - n_gate), F32)], axis=1)
    o_ref[:, o2 + mq:] = gates.astype(BF16)


def gdn_weight_prep(w_in_all, layer, *, o2, n_gate, tr):
    _, d, n_in = w_in_all.shape
    n_out = n_in - n_gate + LANES
    kern = functools.partial(_gdn_weight_prep_kernel, o2=o2, n_gate=n_gate)
    return pl.pallas_call(
        kern,
        grid=(d // tr,),
        in_specs=[pl.BlockSpec((None, tr, n_in), lambda i: (layer, i, 0))],
        out_specs=pl.BlockSpec((tr, n_out), lambda i: (i, 0)),
        out_shape=jax.ShapeDtypeStruct((d, n_out), BF16),
        compiler_params=_cparams(("parallel",)),
        name="gdn_weight_prep",
    )(w_in_all)


def _level_masks():
    ri = lax.broadcasted_iota(jnp.int32, (CHUNK, CHUNK), 0)
    ci = lax.broadcasted_iota(jnp.int32, (CHUNK, CHUNK), 1)
    masks = []
    for l in range(1, 7):
        same = (ri >> l) == (ci >> l)
        diff_half = (ri >> (l - 1)) != (ci >> (l - 1))
        masks.append((same & diff_half & (ri > ci)).astype(F32))
    return ri, ci, masks


def _bmm(a, b):
    return jnp.einsum("gmk,gkn->gmn", a, b, preferred_element_type=F32)


def _unit_lower_inverse(lmat, ri, ci, masks):
    eye = (ri == ci).astype(F32)
    p = eye - lmat * masks[0]
    for l in range(1, 6):
        e = (lmat * masks[l]).astype(BF16)
        pb = p.astype(BF16)
        pe = _bmm(pb, e)
        p = p - _bmm(pe.astype(BF16), pb)
    return p


def _gdn_kernel(x0_ref, xn_ref, gx_ref, w_ref, cw_ref, alog_ref, dtb_ref, ng_ref, o_ref, mq_ref,
                state_ref, carry_ref, proj_ref, next_ref, *, rows):
    t = pl.program_id(1)
    nc = rows // CHUNK
    nh = GDN_HEADS
    hw = nh * GDN_D
    mw = MEM_HEADS * MEM_DH

    def project(x_ref):
        hx = _rms(x_ref[...], gx_ref[...]).astype(BF16)
        return jnp.dot(hx, w_ref[...], preferred_element_type=F32)

    @pl.when(t == 0)
    def _():
        state_ref[...] = jnp.zeros_like(state_ref)
        carry_ref[...] = jnp.zeros_like(carry_ref)
        proj_ref[...] = project(x0_ref)

    next_ref[...] = project(xn_ref)
    qkv_ref = proj_ref.at[:, :3 * hw]
    z_ref = proj_ref.at[:, 3 * hw:4 * hw]
    ba_ref = proj_ref.at[:, 4 * hw + mw:]
    mq_ref[...] = proj_ref[:, 4 * hw:4 * hw + mw]

    ri, ci, masks = _level_masks()
    tril = ri >= ci
    strict = (ri > ci).astype(F32)

    ba = ba_ref[...]
    beta_all = _sigmoid(ba)
    sp_in = ba + dtb_ref[...]
    softplus = jnp.maximum(sp_in, 0.0) + jnp.log(1.0 + jnp.exp(-jnp.abs(sp_in)))
    g_all = -jnp.exp(alog_ref[...]) * softplus
    ti = lax.broadcasted_iota(jnp.int32, (rows, rows), 0)
    tj = lax.broadcasted_iota(jnp.int32, (rows, rows), 1)
    tri = (((ti // CHUNK) == (tj // CHUNK)) & (tj <= ti)).astype(BF16)
    g_hi = g_all.astype(BF16)
    rem = g_all - g_hi.astype(F32)
    g_mid = rem.astype(BF16)
    g_lo = (rem - g_mid.astype(F32)).astype(BF16)
    gsum = jnp.dot(tri, jnp.concatenate([g_hi, g_mid, g_lo], axis=1), preferred_element_type=F32)
    gc_all = gsum[:, :LANES] + gsum[:, LANES:2 * LANES] + gsum[:, 2 * LANES:]
    gc_t = jnp.transpose(gc_all)

    row8 = lax.broadcasted_iota(jnp.int32, (8, GDN_D), 0)

    def conv_silu(col):
        x = qkv_ref[:, col:col + GDN_D]
        c8 = carry_ref[:, col:col + GDN_D]
        acc = x * cw_ref[CONV_K - 1:CONV_K, col:col + GDN_D]
        for k in range(1, CONV_K):
            xr = pltpu.roll(x, k, axis=0)
            cr = pltpu.roll(c8, k, axis=0)
            head = jnp.where(row8 < k, cr, xr[:8])
            sh = jnp.concatenate([head, xr[8:]], axis=0)
            acc = acc + sh * cw_ref[CONV_K - 1 - k:CONV_K - k, col:col + GDN_D]
        return _silu_of_half(acc)

    parts = {name: [] for name in ("q", "k", "kb", "rhs", "qg", "kg", "gcb", "gcrow", "glast")}
    for h in range(nh):
        q = conv_silu(h * GDN_D)
        k = conv_silu(hw + h * GDN_D)
        v = conv_silu(2 * hw + h * GDN_D)
        q = q * lax.rsqrt(jnp.sum(q * q, axis=-1, keepdims=True) + EPS) * (GDN_D ** -0.5)
        k = k * lax.rsqrt(jnp.sum(k * k, axis=-1, keepdims=True) + EPS)
        beta = jnp.broadcast_to(beta_all[:, h:h + 1], (rows, GDN_D))
        gc = jnp.broadcast_to(gc_all[:, nh + h:nh + h + 1], (rows, GDN_D))
        gc3 = gc.reshape(nc, CHUNK, GDN_D)
        glast = jnp.broadcast_to(gc3[:, CHUNK - 1:CHUNK, :], (nc, CHUNK, GDN_D))
        eg = jnp.exp(gc)
        kb = k * beta
        three = lambda a: a.reshape(nc, CHUNK, a.shape[-1])
        parts["q"].append(three(q.astype(BF16)))
        parts["k"].append(three(k.astype(BF16)))
        parts["kb"].append(three(kb.astype(BF16)))
        parts["rhs"].append(three(jnp.concatenate([v * beta, kb * eg], axis=1).astype(BF16)))
        parts["qg"].append(three((q * eg).astype(BF16)))
        parts["kg"].append(three(k) * jnp.exp(glast - gc3))
        parts["gcb"].append(gc3[:, :, :CHUNK])
        parts["glast"].append(glast[:, :1, :])
        for c in range(nc):
            parts["gcrow"].append(
                gc_t[nh + h:nh + h + 1, c * CHUNK:(c + 1) * CHUNK].reshape(1, 1, CHUNK))
    st = {name: jnp.concatenate(v, axis=0) for name, v in parts.items()}

    decay = jnp.exp(jnp.where(tril, st["gcb"] - st["gcrow"], -jnp.inf))
    kq = jnp.einsum("gmd,gnd->gmn", jnp.concatenate([st["kb"], st["q"]], axis=1), st["k"],
                    preferred_element_type=F32)
    lmat = kq[:, :CHUNK] * (decay * strict)
    intra = (kq[:, CHUNK:] * decay).astype(BF16)
    tinv = _unit_lower_inverse(lmat, ri, ci, masks)
    sol = _bmm(tinv.astype(BF16), st["rhs"])

    four = lambda a: a.reshape((nh, nc) + a.shape[1:])
    u4 = four(sol[:, :, :GDN_D])
    w4 = four(sol[:, :, GDN_D:].astype(BF16))
    qg4, kg4, intra4 = four(st["qg"]), four(st["kg"]), four(intra)
    egl4 = four(jnp.exp(st["glast"]))
    zg = _silu_of_half(z_ref[...])
    state = state_ref[...]
    for c in range(nc):
        wq = jnp.concatenate([w4[:, c], qg4[:, c]], axis=1)
        ws_qs = _bmm(wq, state.astype(BF16))
        v_new = (u4[:, c] - ws_qs[:, :CHUNK]).astype(BF16)
        o_c = ws_qs[:, CHUNK:] + _bmm(intra4[:, c], v_new)
        kgt = jnp.stack([jnp.transpose(kg4[h, c]) for h in range(nh)], axis=0).astype(BF16)
        state = state * egl4[:, c] + _bmm(kgt, v_new)
        sl = slice(c * CHUNK, (c + 1) * CHUNK)
        for h in range(nh):
            cs = slice(h * GDN_D, (h + 1) * GDN_D)
            o_ref[sl, cs] = _rms(o_c[h], ng_ref[...]) * zg[sl, cs]
    state_ref[...] = state
    carry_ref[...] = qkv_ref[rows - 8:rows, :]
    proj_ref[...] = next_ref[...]


def gdn(x, gx, w, conv_w, a_log_p, dt_bias_p, norm_g, *, batch, seq, rows):
    t, d = x.shape
    nt = seq // rows
    hw = GDN_HEADS * GDN_D
    mw = MEM_HEADS * MEM_DH
    n = w.shape[1]
    kern = functools.partial(_gdn_kernel, rows=rows)
    const = lambda b, i: (0, 0)
    row = lambda b, i: (b * nt + i, 0)
    return pl.pallas_call(
        kern,
        grid=(batch, nt),
        in_specs=[pl.BlockSpec((rows, d), lambda b, i: (b * nt, 0)),
                  pl.BlockSpec((rows, d), lambda b, i: (b * nt + jnp.minimum(i + 1, nt - 1), 0)),
                  pl.BlockSpec((1, d), const),
                  pl.BlockSpec((d, n), const, pipeline_mode=pl.Buffered(1)),
                  pl.BlockSpec((CONV_K, 3 * hw), const),
                  pl.BlockSpec((1, LANES), const),
                  pl.BlockSpec((1, LANES), const),
                  pl.BlockSpec((1, GDN_D), const)],
        out_specs=[pl.BlockSpec((rows, hw), row), pl.BlockSpec((rows, mw), row)],
        out_shape=[jax.ShapeDtypeStruct((t, hw), F32), jax.ShapeDtypeStruct((t, mw), F32)],
        scratch_shapes=[pltpu.VMEM((GDN_HEADS, GDN_D, GDN_D), F32),
                        pltpu.VMEM((8, 3 * hw), F32),
                        pltpu.VMEM((rows, n), F32),
                        pltpu.VMEM((rows, n), F32)],
        compiler_params=_cparams(("arbitrary", "arbitrary")),
        name="gdn",
    )(x, x, gx.reshape(1, d), w, conv_w, a_log_p, dt_bias_p, norm_g.reshape(1, GDN_D))


def _swa_kernel(sink_ref, x_ref, g_ref, wq_ref, cos_ref, sin_ref, kvc_ref, kvp_ref, o_ref, mq_ref, q_ref,
                *, qrows):
    t = pl.program_id(1)
    qw = SWA_HEADS * SWA_DH
    hx = _rms(x_ref[...], g_ref[...]).astype(BF16)
    q_ref[...] = jnp.dot(hx, wq_ref[...], preferred_element_type=F32)
    mq_ref[...] = q_ref[:, qw:]
    nsub = qrows // SWA_BLOCK
    lane = lax.broadcasted_iota(jnp.int32, (1, LANES), 1)
    lo = lane < SWA_DH
    qi = lax.broadcasted_iota(jnp.int32, (SWA_BLOCK, 2 * SWA_BLOCK), 0)
    ki = lax.broadcasted_iota(jnp.int32, (SWA_BLOCK, 2 * SWA_BLOCK), 1)
    band = (ki > qi) & (ki <= qi + SWA_BLOCK)
    band_first = band & (ki >= jnp.where(t > 0, 0, SWA_BLOCK))
    dn = (((1,), (1,)), ((), ()))
    pairs_per_kv = SWA_HEADS // SWA_KV_HEADS // 2
    ones = jnp.ones((2 * SWA_BLOCK, LANES), BF16)

    kv_all = jnp.concatenate([kvp_ref[...], kvc_ref[...]], axis=0)
    kblk = kv_all[:, :LANES]
    vblk = kv_all[:, LANES:]
    kroll = pltpu.roll(kblk, SWA_DH, axis=1)
    vroll = pltpu.roll(vblk, SWA_DH, axis=1)
    kexp = [(jnp.where(lo, kblk, 0.0).astype(BF16), jnp.where(lo, 0.0, kroll).astype(BF16)),
            (jnp.where(lo, kroll, 0.0).astype(BF16), jnp.where(lo, 0.0, kblk).astype(BF16))]
    vexp = [(jnp.where(lo, vblk, 0.0).astype(BF16), jnp.where(lo, 0.0, vroll).astype(BF16)),
            (jnp.where(lo, vroll, 0.0).astype(BF16), jnp.where(lo, 0.0, vblk).astype(BF16))]

    for j in range(nsub):
        ws = slice(j * SWA_BLOCK, (j + 2) * SWA_BLOCK)
        rs = slice(j * SWA_BLOCK, (j + 1) * SWA_BLOCK)
        mask = band_first if j == 0 else band
        cosp, sinp = cos_ref[rs, :], sin_ref[rs, :]
        groups = [(g, half) for g in range(SWA_KV_HEADS) for half in range(2)]
        pairs = lambda g: [g * pairs_per_kv + i for i in range(pairs_per_kv)]
        qs = [jnp.concatenate(
            [(_rope(q_ref[rs, p * LANES:(p + 1) * LANES], cosp, sinp) * (SWA_DH ** -0.5)).astype(BF16)
             for p in pairs(g)], axis=0) for g in range(SWA_KV_HEADS)]
        scs = [lax.dot_general(qs[g], kexp[g][half][ws], dn, preferred_element_type=F32)
               for g, half in groups]
        es, sink_terms = [], []
        for (g, half), sc in zip(groups, scs):
            e_g, s_g = [], []
            for i, p in enumerate(pairs(g)):
                sink = sink_ref[2 * p + half]
                s_i = jnp.where(mask, sc[i * SWA_BLOCK:(i + 1) * SWA_BLOCK], -jnp.inf)
                m = jnp.maximum(jnp.max(s_i, axis=-1, keepdims=True), sink)
                e_g.append(jnp.exp(s_i - m).astype(BF16))
                s_g.append(jnp.broadcast_to(jnp.exp(sink - m), (SWA_BLOCK, LANES)))
            es.append(jnp.concatenate(e_g, axis=0))
            sink_terms.append(jnp.concatenate(s_g, axis=0))
        pvs = []
        for n, (g, half) in enumerate(groups):
            r = jnp.dot(es[n], jnp.concatenate([vexp[g][half][ws], ones], axis=1),
                        preferred_element_type=F32)
            pvs.append(r[:, :LANES] / (r[:, LANES:] + sink_terms[n]))
        for g in range(SWA_KV_HEADS):
            acc = pvs[2 * g] + pvs[2 * g + 1]
            for i, p in enumerate(pairs(g)):
                o_ref[rs, p * LANES:(p + 1) * LANES] = acc[i * SWA_BLOCK:(i + 1) * SWA_BLOCK]


def swa(x, g, wq, layer, kv, sinks, cosp, sinp, *, batch, seq, qrows):
    t, d = x.shape
    nt = seq // qrows
    qw = SWA_HEADS * SWA_DH
    n = wq.shape[-1]
    nsub = qrows // SWA_BLOCK
    nblk = seq // SWA_BLOCK
    row = lambda b, i: (b * nt + i, 0)
    const = lambda b, i: (0, 0)
    kern = functools.partial(_swa_kernel, qrows=qrows)
    return pl.pallas_call(
        kern,
        grid=(batch, nt),
        in_specs=[pl.BlockSpec(memory_space=pltpu.SMEM),
                  pl.BlockSpec((qrows, d), row),
                  pl.BlockSpec((1, d), const),
                  pl.BlockSpec((None, d, n), lambda b, i: (layer, 0, 0)),
                  pl.BlockSpec((qrows, LANES), row), pl.BlockSpec((qrows, LANES), row),
                  pl.BlockSpec((qrows, 2 * LANES), row),
                  pl.BlockSpec((SWA_BLOCK, 2 * LANES),
                               lambda b, i: (b * nblk + jnp.maximum(i * nsub - 1, 0), 0))],
        out_specs=[pl.BlockSpec((qrows, qw), row), pl.BlockSpec((qrows, n - qw), row)],
        out_shape=[jax.ShapeDtypeStruct((t, qw), F32), jax.ShapeDtypeStruct((t, n - qw), F32)],
        scratch_shapes=[pltpu.VMEM((qrows, n), F32)],
        compiler_params=_cparams(("parallel", "arbitrary")),
        name="swa",
    )(sinks, x, g.reshape(1, d), wq, cosp, sinp, kv, kv)


def _mix_ffn_kernel(x_ref, mix_ref, mq_ref, mkv_ref, wo_ref, g_ref, wg_ref, wu_ref, wd_ref, gf_ref,
                    *rest, ff_tile, final, with_kv):
    if with_kv:
        gkv_ref, wkv_ref, cos_ref, sin_ref, o_ref, kv_ref, acc_ref = rest
    else:
        o_ref, acc_ref = rest
    lane = lax.broadcasted_iota(jnp.int32, (1, LANES), 1)
    lo = lane < MEM_DH
    mw = MEM_HEADS * MEM_DH
    dn = (((1,), (1,)), ((), ()))
    outs = [mix_ref[...].astype(BF16)]
    for p in range(MEM_HEADS // 2):
        qp = (mq_ref[:, p * LANES:(p + 1) * LANES] * (MEM_DH ** -0.5)).astype(BF16)
        kp = mkv_ref[:, p * LANES:(p + 1) * LANES]
        vp = mkv_ref[:, mw + p * LANES:mw + (p + 1) * LANES]
        acc = None
        for half in range(2):
            sel = lo if half == 0 else jnp.logical_not(lo)
            kh = jnp.where(sel, kp, 0.0).astype(BF16)
            vh = jnp.where(sel, vp, 0.0).astype(BF16)
            sc = lax.dot_general(qp, kh, dn, preferred_element_type=F32)
            m = jnp.max(sc, axis=-1, keepdims=True)
            e = jnp.exp(sc - m)
            scale = jnp.broadcast_to(1.0 / jnp.sum(e, axis=-1, keepdims=True), (e.shape[0], LANES))
            pv = jnp.dot(e.astype(BF16), vh, preferred_element_type=F32) * scale
            acc = pv if acc is None else acc + pv
        outs.append(acc.astype(BF16))
    cat = jnp.concatenate(outs, axis=1)
    x = x_ref[...] + jnp.dot(cat, wo_ref[...], preferred_element_type=F32)
    h = _rms(x, g_ref[...]).astype(BF16)
    d_ff = wg_ref.shape[1]
    acc_ref[...] = x
    for j in range(d_ff // ff_tile):
        cs = slice(j * ff_tile, (j + 1) * ff_tile)
        gate = jnp.dot(h, wg_ref[:, cs], preferred_element_type=F32)
        up = jnp.dot(h, wu_ref[:, cs], preferred_element_type=F32)
        act = (_silu(gate) * up).astype(BF16)
        acc_ref[...] += jnp.dot(act, wd_ref[cs, :], preferred_element_type=F32)
    y = acc_ref[...]
    o_ref[...] = _rms(y, gf_ref[...]) if final else y
    if with_kv:
        kv = jnp.dot(_rms(y, gkv_ref[...]).astype(BF16), wkv_ref[...], preferred_element_type=F32)
        kv_ref[:, :LANES] = _rope(kv[:, :LANES], cos_ref[...], sin_ref[...])
        kv_ref[:, LANES:] = kv[:, LANES:]


def mix_ffn(x, mix, mem_q, mkv, w_out, g, wgu, wd, g_final, *, layer, batch, seq, tm, ff_tile, final, kv_args=None):
    t, d = x.shape
    nt = seq // tm
    mixw = mix.shape[1]
    mw = MEM_HEADS * MEM_DH
    mlen = mkv.shape[0] // batch
    d_ff = wd.shape[1]
    row = lambda b, i: (b * nt + i, 0)
    const = lambda b, i: (0, 0)
    single = pl.Buffered(1)
    with_kv = kv_args is not None
    in_specs = [pl.BlockSpec((tm, d), row),
                pl.BlockSpec((tm, mixw), row),
                pl.BlockSpec((tm, mw), row),
                pl.BlockSpec((mlen, 2 * mw), lambda b, i: (b, 0)),
                pl.BlockSpec((None, mixw + mw, d), lambda b, i: (layer, 0, 0), pipeline_mode=single),
                pl.BlockSpec((1, d), const),
                pl.BlockSpec((None, d, d_ff), lambda b, i: (layer, 0, 0), pipeline_mode=single),
                pl.BlockSpec((None, d, d_ff), lambda b, i: (layer, 0, 1), pipeline_mode=single),
                pl.BlockSpec((None, d_ff, d), lambda b, i: (layer, 0, 0), pipeline_mode=single),
                pl.BlockSpec((1, d), const)]
    args = [x, mix, mem_q, mkv, w_out, g.reshape(1, d), wgu, wgu, wd, g_final.reshape(1, d)]
    out_specs = pl.BlockSpec((tm, d), row)
    out_shape = jax.ShapeDtypeStruct((t, d), F32)
    if with_kv:
        ln_kv, w_kv, cosp, sinp = kv_args
        nkv = w_kv.shape[1]
        in_specs += [pl.BlockSpec((1, d), const), pl.BlockSpec((d, nkv), const),
                     pl.BlockSpec((tm, LANES), row), pl.BlockSpec((tm, LANES), row)]
        args += [ln_kv.reshape(1, d), w_kv, cosp, sinp]
        out_specs = [out_specs, pl.BlockSpec((tm, nkv), row)]
        out_shape = [out_shape, jax.ShapeDtypeStruct((t, nkv), F32)]
    kern = functools.partial(_mix_ffn_kernel, ff_tile=ff_tile, final=final, with_kv=with_kv)
    return pl.pallas_call(
        kern,
        grid=(batch, nt),
        in_specs=in_specs,
        out_specs=out_specs,
        out_shape=out_shape,
        scratch_shapes=[pltpu.VMEM((tm, d), F32)],
        compiler_params=_cparams(("parallel", "parallel")),
        name="mix_ffn",
    )(*args)


def _rope_table_kernel(pos_ref, inv_ref, cos_ref, sin_ref):
    half = ROT_DIM // 2
    lane = lax.broadcasted_iota(jnp.int32, (1, LANES), 1) & (SWA_DH - 1)
    for r in range(pos_ref.shape[0]):
        prow = pos_ref[r:r + 1, :].astype(F32)
        pcol = jnp.transpose(jnp.broadcast_to(prow, (LANES, LANES)))
        ang = pcol * inv_ref[...]
        sin = jnp.sin(ang)
        rs = slice(r * LANES, (r + 1) * LANES)
        cos_ref[rs, :] = jnp.where(lane < ROT_DIM, jnp.cos(ang), 1.0)
        sin_ref[rs, :] = jnp.where(lane < half, -sin, jnp.where(lane < ROT_DIM, sin, 0.0))


def rope_tables(positions):
    half = ROT_DIM // 2
    t = positions.size
    rows = 8
    inv = ROPE_THETA ** (-jnp.arange(0, ROT_DIM, 2, dtype=F32) / ROT_DIM)
    inv_lane = inv[(jnp.arange(LANES) % SWA_DH) % half].reshape(1, LANES)
    out = jax.ShapeDtypeStruct((t, LANES), F32)
    return pl.pallas_call(
        _rope_table_kernel,
        grid=(t // (rows * LANES),),
        in_specs=[pl.BlockSpec((rows, LANES), lambda i: (i, 0)), pl.BlockSpec((1, LANES), lambda i: (0, 0))],
        out_specs=[pl.BlockSpec((rows * LANES, LANES), lambda i: (i, 0))] * 2,
        out_shape=[out, out],
        compiler_params=_cparams(("parallel",)),
        name="rope_tables",
    )(positions.reshape(t // LANES, LANES), inv_lane)


def _pad_lanes(v, offset):
    out = jnp.zeros((1, LANES), F32)
    return lax.dynamic_update_slice(out, v.reshape(1, -1).astype(F32), (0, offset))


def kernel(x, mem, positions, ln_mix, ln_ffn, ln_mem, w_mem_kv, w_out, w_gate_up, w_down,
           gdn_w_in, gdn_conv, gdn_A_log, gdn_dt_bias, gdn_norm,
           swa_w_q, swa_sinks, ln_kv, w_kv, ln_final):
    batch, seq, d = x.shape
    depth = ln_mix.shape[0]
    n_a = gdn_w_in.shape[0]
    t = batch * seq
    mlen = mem.shape[1]
    hw = GDN_HEADS * GDN_D

    cosp, sinp = rope_tables(positions)
    xs = x.reshape(t, d)
    mem2 = mem.reshape(batch * mlen, d)
    w_mem_kv, w_out, w_gate_up, w_down, swa_w_q = (
        w.astype(BF16) for w in (w_mem_kv, w_out, w_gate_up, w_down, swa_w_q))
    kv = None
    for layer in range(depth):
        mkv = norm_matmul(mem2, ln_mem, w_mem_kv, layer=layer, tm=batch * mlen)
        if layer < n_a:
            a = layer
            o2 = 4 * hw
            w_r = gdn_weight_prep(gdn_w_in, a, o2=o2, n_gate=2 * GDN_HEADS, tr=256)
            mix, proj = gdn(xs, ln_mix[layer], w_r, 0.5 * gdn_conv[a], _pad_lanes(gdn_A_log[a], GDN_HEADS),
                            _pad_lanes(gdn_dt_bias[a], GDN_HEADS), gdn_norm[a],
                            batch=batch, seq=seq, rows=256)
        else:
            bl = layer - n_a
            mix, proj = swa(xs, ln_mix[layer], swa_w_q, bl, kv, swa_sinks[bl], cosp, sinp,
                            batch=batch, seq=seq, qrows=512)
        with_kv = layer == n_a - 1
        out = mix_ffn(xs, mix, proj, mkv, w_out, ln_ffn[layer], w_gate_up, w_down, ln_final,
                      layer=layer, batch=batch, seq=seq, tm=512 if with_kv else 1024, ff_tile=256,
                      final=(layer == depth - 1),
                      kv_args=(ln_kv, w_kv.astype(BF16), cosp, sinp) if with_kv else None)
        xs, kv = out if with_kv else (out, kv)
    return xs.reshape(batch, seq, d)
```
